```python
import math
import jax, jax.numpy as jnp
from jax import lax
import numpy as np

D_MODEL = 4096
BATCH = 4
SEQ = 2048
DEPTH = 4
DEC_BATCH = 128
DEC_SEQ = 1
PAST_LEN = 8192
PAGE_SIZE = 128

SC_DIM = D_MODEL // 2
SC_WIDTH = 3
SSD_HEADDIM = 64
SSD_DIM = D_MODEL // 2
SSD_HEADS = SSD_DIM // SSD_HEADDIM
SSD_GROUPS = 8
SSD_STATE = 128
SSD_CONV = 4
SSD_CHUNK = 128
SSD_CONV_DIM = SSD_DIM + 2 * SSD_GROUPS * SSD_STATE
MLA_HEADS = 16
MLA_NOPE = 128
MLA_ROPE = 64
MLA_V = 128
MLA_Q_LORA = D_MODEL // 4
MLA_KV_LORA = D_MODEL // 16
MLA_DIM = MLA_HEADS * MLA_V
MLA_SCALE = (MLA_NOPE + MLA_ROPE) ** -0.5
ROPE_THETA = 10000.0
Q_BLOCK = 128
D_FF = 4 * D_MODEL
N_BRANCH = 3
EPS = 1e-6

IN_SIZES = (SC_DIM, SC_DIM, SC_DIM, SSD_DIM, SSD_CONV_DIM, SSD_HEADS,
            MLA_Q_LORA, MLA_KV_LORA, MLA_ROPE, N_BRANCH * D_MODEL)
IN_COLS = 3 * SC_DIM + SSD_DIM + SSD_CONV_DIM + SSD_HEADS + MLA_Q_LORA + MLA_KV_LORA + MLA_ROPE + N_BRANCH * D_MODEL

kernel_name = 'hybrid_conv_ssd_mla_gated_decoder_step'


def split_cols(a, sizes):
    out, start = [], 0
    for s in sizes:
        out.append(a[..., start:start + s])
        start += s
    return out


def rmsnorm(x, w):
    xf = x.astype(jnp.float32)
    y = xf * lax.rsqrt(jnp.mean(xf * xf, axis=-1, keepdims=True) + EPS)
    return (y * w.astype(jnp.float32)).astype(x.dtype)


def group_rmsnorm(y, w, groups):
    shp = y.shape
    yf = y.astype(jnp.float32).reshape(shp[:-1] + (groups, shp[-1] // groups))
    yf = yf * lax.rsqrt(jnp.mean(yf * yf, axis=-1, keepdims=True) + EPS)
    return (yf.reshape(shp) * w.astype(jnp.float32)).astype(y.dtype)


def apply_rope(x, pos):
    half = x.shape[-1] // 2
    inv = ROPE_THETA ** (-jnp.arange(half, dtype=jnp.float32) / half)
    ang = pos.astype(jnp.float32)[:, None] * inv[None, :]
    shape = (1, pos.shape[0]) + (1,) * (x.ndim - 3) + (half,)
    cos = jnp.cos(ang).reshape(shape)
    sin = jnp.sin(ang).reshape(shape)
    xf = x.astype(jnp.float32)
    x1, x2 = xf[..., :half], xf[..., half:]
    return jnp.concatenate([x1 * cos - x2 * sin, x1 * sin + x2 * cos], axis=-1).astype(x.dtype)


def causal_dwconv(u, buf, w):
    width = w.shape[0]
    L = u.shape[1]
    full = jnp.concatenate([buf.astype(u.dtype), u], axis=1)
    y = sum(full[:, k:k + L] * w[k] for k in range(width))
    return y, full[:, full.shape[1] - (width - 1):]


def ssd_chunked(x, dt, A, Bm, Cm, h0):
    f32 = jnp.float32
    b, L, H, P = x.shape
    G, N = Bm.shape[-2], Bm.shape[-1]
    E = H // G
    Q = SSD_CHUNK if L % SSD_CHUNK == 0 else L
    nc = L // Q
    xc = x.astype(f32).reshape(b, nc, Q, G, E, P)
    dtc = dt.astype(f32).reshape(b, nc, Q, G, E)
    Bc = Bm.astype(f32).reshape(b, nc, Q, G, N)
    Cc = Cm.astype(f32).reshape(b, nc, Q, G, N)
    acum = jnp.cumsum(dtc * A.astype(f32).reshape(G, E), axis=2)
    tri = jnp.tril(jnp.ones((Q, Q), dtype=bool))[:, :, None, None]
    seg = acum[:, :, :, None] - acum[:, :, None, :]
    decay = jnp.exp(jnp.where(tri, seg, -jnp.inf))
    cb = jnp.einsum('bctgn,bcsgn->bctsg', Cc, Bc)
    w_ts = cb[..., None] * decay * dtc[:, :, None]
    y_intra = jnp.einsum('bctsge,bcsgep->bctgep', w_ts, xc)
    to_end = jnp.exp(acum[:, :, -1:] - acum) * dtc
    s_chunk = jnp.einsum('bcsge,bcsgn,bcsgep->bcgepn', to_end, Bc, xc)
    chunk_decay = jnp.exp(acum[:, :, -1])

    def step(hc, inp):
        dec, sc = inp
        return hc * dec[..., None, None] + sc, hc

    h_init = h0.astype(f32).reshape(b, G, E, P, N)
    h_last, h_starts = lax.scan(step, h_init, (jnp.moveaxis(chunk_decay, 1, 0), jnp.moveaxis(s_chunk, 1, 0)))
    h_starts = jnp.moveaxis(h_starts, 0, 1)
    y_inter = jnp.einsum('bctgn,bcgepn->bctgep', Cc, h_starts) * jnp.exp(acum)[..., None]
    y = (y_intra + y_inter).reshape(b, L, H, P)
    return y.astype(x.dtype), h_last.reshape(b, H, P, N).astype(h0.dtype)


def mla_attend_prompt(q_lat, q_rope, lat, k_rope):
    b, S, H, R = q_lat.shape
    nb = S // Q_BLOCK
    ql = jnp.moveaxis(q_lat.reshape(b, nb, Q_BLOCK, H, R), 1, 0)
    qr = jnp.moveaxis(q_rope.reshape(b, nb, Q_BLOCK, H, MLA_ROPE), 1, 0)
    kpos = jnp.arange(S)

    def block(args):
        qlb, qrb, i = args
        qpos = i * Q_BLOCK + jnp.arange(Q_BLOCK)
        s = jnp.einsum('bqhr,bkr->bhqk', qlb, lat) + jnp.einsum('bqhd,bkd->bhqk', qrb, k_rope)
        s = jnp.where(kpos[None, :] <= qpos[:, None], s.astype(jnp.float32) * MLA_SCALE, -jnp.inf)
        pr = jax.nn.softmax(s, axis=-1).astype(lat.dtype)
        return jnp.einsum('bhqk,bkr->bqhr', pr, lat)

    o = lax.map(block, (ql, qr, jnp.arange(nb)))
    return jnp.moveaxis(o, 0, 1).reshape(b, S, H, R)


def mla_attend_sample(q_lat, q_rope, lat_new, kr_new, lat_past, kr_past):
    T = lat_past.shape[1]
    Lq = q_lat.shape[1]
    s_past = jnp.einsum('bqhr,btr->bhqt', q_lat, lat_past) + jnp.einsum('bqhd,btd->bhqt', q_rope, kr_past)
    s_new = jnp.einsum('bqhr,bkr->bhqk', q_lat, lat_new) + jnp.einsum('bqhd,bkd->bhqk', q_rope, kr_new)
    causal = jnp.tril(jnp.ones((Lq, Lq), dtype=bool))
    s_new = jnp.where(causal, s_new.astype(jnp.float32), -jnp.inf)
    s = jnp.concatenate([s_past.astype(jnp.float32), s_new], axis=-1) * MLA_SCALE
    pr = jax.nn.softmax(s, axis=-1).astype(lat_new.dtype)
    return (jnp.einsum('bhqt,btr->bqhr', pr[..., :T], lat_past)
            + jnp.einsum('bhqk,bkr->bqhr', pr[..., T:], lat_new))


def trunk_layer(x, pos, sc_buf, ssd_buf, ssd_h0, past, p):
    b, L, _ = x.shape
    h = rmsnorm(x, p['norm_mix_w'])
    proj = h @ p['w_in']
    (sc_b, sc_c, sc_x, ssd_z, ssd_xbc, ssd_dt,
     mla_cq, mla_ckv, mla_kr, gate_pre) = split_cols(proj, IN_SIZES)

    v, sc_buf_new = causal_dwconv(sc_c * sc_x, sc_buf, p['sc_conv_w'])
    y_sc = (sc_b * v) @ p['sc_w_out']

    xbc, ssd_buf_new = causal_dwconv(ssd_xbc, ssd_buf, p['ssd_conv_w'])
    xbc = jax.nn.silu(xbc + p['ssd_conv_b'])
    xs, Bm, Cm = split_cols(xbc, (SSD_DIM, SSD_GROUPS * SSD_STATE, SSD_GROUPS * SSD_STATE))
    xs = xs.reshape(b, L, SSD_HEADS, SSD_HEADDIM)
    Bm = Bm.reshape(b, L, SSD_GROUPS, SSD_STATE)
    Cm = Cm.reshape(b, L, SSD_GROUPS, SSD_STATE)
    dt = jax.nn.softplus(ssd_dt.astype(jnp.float32) + p['ssd_dt_bias'].astype(jnp.float32))
    A = -jnp.exp(p['ssd_A_log'].astype(jnp.float32))
    y, ssd_h_new = ssd_chunked(xs, dt, A, Bm, Cm, ssd_h0)
    y = (y + xs * p['ssd_D'][:, None]).reshape(b, L, SSD_DIM)
    y = group_rmsnorm(y * jax.nn.silu(ssd_z), p['ssd_norm_w'], SSD_GROUPS)
    y_ssd = y @ p['ssd_w_out']

    q = (rmsnorm(mla_cq, p['mla_q_norm_w']) @ p['mla_w_q_up']).reshape(b, L, MLA_HEADS, MLA_NOPE + MLA_ROPE)
    q_lat = jnp.einsum('blhd,rhd->blhr', q[..., :MLA_NOPE], p['mla_w_uk'])
    q_rope = apply_rope(q[..., MLA_NOPE:], pos)
    lat = rmsnorm(mla_ckv, p['mla_kv_norm_w'])
    k_rope = apply_rope(mla_kr, pos)
    if past is None:
        o_lat = mla_attend_prompt(q_lat, q_rope, lat, k_rope)
    else:
        o_lat = mla_attend_sample(q_lat, q_rope, lat, k_rope, past[0], past[1])
    o = jnp.einsum('blhr,rhv->blhv', o_lat, p['mla_w_uv']).reshape(b, L, MLA_DIM)
    y_mla = o @ p['mla_w_o']

    g = jax.nn.sigmoid(gate_pre.reshape(b, L, N_BRANCH, D_MODEL) + p['gate_b'])
    merged = g[:, :, 0] * y_sc + g[:, :, 1] * y_ssd + g[:, :, 2] * y_mla
    x = x + merged @ p['w_out']

    h2 = rmsnorm(x, p['norm_mlp_w'])
    x = x + jnp.square(jax.nn.relu(h2 @ p['mlp_w1'])) @ p['mlp_w2']
    return x, (lat, k_rope, sc_buf_new, ssd_buf_new, ssd_h_new)


def setup_inputs(seed: int = 0) -> dict:
    key = jax.random.key(seed)
    keys = jax.random.split(key, 48)
    ctr = [0]
    f32 = jnp.float32

    def nk():
        ctr[0] += 1
        return keys[ctr[0] - 1]

    def nrm(shape, scale=1.0):
        return jax.random.normal(nk(), shape, f32) * scale

    def gain(shape):
        return 1.0 + nrm(shape, 0.02)

    n_pages = PAST_LEN // PAGE_SIZE
    n_pool = (5 * DEC_BATCH * n_pages) // 4
    perm = jax.random.permutation(nk(), n_pool)
    page_table = perm[:DEC_BATCH * n_pages].reshape(DEC_BATCH, n_pages).astype(jnp.int32)

    dt0 = jnp.exp(jax.random.uniform(nk(), (DEPTH, SSD_HEADS), f32, math.log(1e-3), math.log(1e-1)))
    a0 = jax.random.uniform(nk(), (DEPTH, SSD_HEADS), f32, 1.0, 16.0)

    return {
        'x_prompt': nrm((BATCH, SEQ, D_MODEL)),
        'x_sample': nrm((DEC_BATCH, DEC_SEQ, D_MODEL)),
        'cache_mla_latent': nrm((n_pool, PAGE_SIZE, DEPTH, MLA_KV_LORA)),
        'cache_mla_krope': nrm((n_pool, PAGE_SIZE, DEPTH, MLA_ROPE)),
        'state_ssd': nrm((DEPTH, DEC_BATCH, SSD_HEADS, SSD_HEADDIM, SSD_STATE), 0.1),
        'state_ssd_conv': nrm((DEPTH, DEC_BATCH, SSD_CONV - 1, SSD_CONV_DIM)),
        'state_short_conv': nrm((DEPTH, DEC_BATCH, SC_WIDTH - 1, SC_DIM)),
        'page_table': page_table,
        'norm_mix_w': gain((DEPTH, D_MODEL)),
        'w_in': nrm((DEPTH, D_MODEL, IN_COLS), D_MODEL ** -0.5),
        'gate_b': nrm((DEPTH, N_BRANCH, D_MODEL), 0.02),
        'sc_conv_w': nrm((DEPTH, SC_WIDTH, SC_DIM), SC_WIDTH ** -0.5),
        'sc_w_out': nrm((DEPTH, SC_DIM, D_MODEL), SC_DIM ** -0.5),
        'ssd_conv_w': nrm((DEPTH, SSD_CONV, SSD_CONV_DIM), SSD_CONV ** -0.5),
        'ssd_conv_b': nrm((DEPTH, SSD_CONV_DIM), 0.02),
        'ssd_dt_bias': jnp.log(jnp.expm1(dt0)),
        'ssd_A_log': jnp.log(a0),
        'ssd_D': gain((DEPTH, SSD_HEADS)),
        'ssd_norm_w': gain((DEPTH, SSD_DIM)),
        'ssd_w_out': nrm((DEPTH, SSD_DIM, D_MODEL), SSD_DIM ** -0.5),
        'mla_q_norm_w': gain((DEPTH, MLA_Q_LORA)),
        'mla_w_q_up': nrm((DEPTH, MLA_Q_LORA, MLA_HEADS * (MLA_NOPE + MLA_ROPE)), MLA_Q_LORA ** -0.5),
        'mla_kv_norm_w': gain((DEPTH, MLA_KV_LORA)),
        'mla_w_uk': nrm((DEPTH, MLA_KV_LORA, MLA_HEADS, MLA_NOPE), MLA_KV_LORA ** -0.5),
        'mla_w_uv': nrm((DEPTH, MLA_KV_LORA, MLA_HEADS, MLA_V), MLA_KV_LORA ** -0.5),
        'mla_w_o': nrm((DEPTH, MLA_DIM, D_MODEL), MLA_DIM ** -0.5),
        'w_out': nrm((DEPTH, D_MODEL, D_MODEL), D_MODEL ** -0.5),
        'norm_mlp_w': gain((DEPTH, D_MODEL)),
        'mlp_w1': nrm((DEPTH, D_MODEL, D_FF), D_MODEL ** -0.5),
        'mlp_w2': nrm((DEPTH, D_FF, D_MODEL), D_FF ** -0.5),
        'final_norm_w': gain((D_MODEL,)),
    }


def reference(x_prompt, x_sample, cache_mla_latent, cache_mla_krope, state_ssd, state_ssd_conv,
              state_short_conv, page_table, norm_mix_w, w_in, gate_b, sc_conv_w, sc_w_out,
              ssd_conv_w, ssd_conv_b, ssd_dt_bias, ssd_A_log, ssd_D, ssd_norm_w, ssd_w_out,
              mla_q_norm_w, mla_w_q_up, mla_kv_norm_w, mla_w_uk, mla_w_uv, mla_w_o, w_out,
              norm_mlp_w, mlp_w1, mlp_w2, final_norm_w):
    bp, sp, _ = x_prompt.shape
    bs, ss, _ = x_sample.shape
    n_pages = page_table.shape[1]
    past_len = n_pages * cache_mla_latent.shape[1]
    pos_prompt = jnp.arange(sp)
    pos_sample = past_len + jnp.arange(ss)
    xp, xs = x_prompt, x_sample
    st_p, st_s = [], []
    for l in range(DEPTH):
        p = {
            'norm_mix_w': norm_mix_w[l], 'w_in': w_in[l], 'gate_b': gate_b[l],
            'sc_conv_w': sc_conv_w[l], 'sc_w_out': sc_w_out[l],
            'ssd_conv_w': ssd_conv_w[l], 'ssd_conv_b': ssd_conv_b[l], 'ssd_dt_bias': ssd_dt_bias[l],
            'ssd_A_log': ssd_A_log[l], 'ssd_D': ssd_D[l], 'ssd_norm_w': ssd_norm_w[l],
            'ssd_w_out': ssd_w_out[l], 'mla_q_norm_w': mla_q_norm_w[l], 'mla_w_q_up': mla_w_q_up[l],
            'mla_kv_norm_w': mla_kv_norm_w[l], 'mla_w_uk': mla_w_uk[l], 'mla_w_uv': mla_w_uv[l],
            'mla_w_o': mla_w_o[l], 'w_out': w_out[l], 'norm_mlp_w': norm_mlp_w[l],
            'mlp_w1': mlp_w1[l], 'mlp_w2': mlp_w2[l],
        }
        sc0 = jnp.zeros((bp, SC_WIDTH - 1, SC_DIM), x_prompt.dtype)
        ssdc0 = jnp.zeros((bp, SSD_CONV - 1, SSD_CONV_DIM), x_prompt.dtype)
        h0 = jnp.zeros((bp, SSD_HEADS, SSD_HEADDIM, SSD_STATE), state_ssd.dtype)
        xp, sp_l = trunk_layer(xp, pos_prompt, sc0, ssdc0, h0, None, p)
        lat_past = cache_mla_latent[page_table, :, l].reshape(bs, past_len, MLA_KV_LORA)
        kr_past = cache_mla_krope[page_table, :, l].reshape(bs, past_len, MLA_ROPE)
        xs, ss_l = trunk_layer(xs, pos_sample, state_short_conv[l], state_ssd_conv[l], state_ssd[l],
                               (lat_past, kr_past), p)
        st_p.append(sp_l)
        st_s.append(ss_l)
    y_prompt = rmsnorm(xp, final_norm_w)
    y_sample = rmsnorm(xs, final_norm_w)
    lat_prompt = jnp.stack([s[0] for s in st_p], axis=2)
    krope_prompt = jnp.stack([s[1] for s in st_p], axis=2)
    lat_sample = jnp.stack([s[0] for s in st_s], axis=2)
    krope_sample = jnp.stack([s[1] for s in st_s], axis=2)
    sc_conv_prompt = jnp.stack([s[2] for s in st_p], axis=0)
    sc_conv_sample = jnp.stack([s[2] for s in st_s], axis=0)
    ssd_conv_prompt = jnp.stack([s[3] for s in st_p], axis=0)
    ssd_conv_sample = jnp.stack([s[3] for s in st_s], axis=0)
    ssd_prompt = jnp.stack([s[4] for s in st_p], axis=0)
    ssd_sample = jnp.stack([s[4] for s in st_s], axis=0)
    return (y_prompt, y_sample, lat_prompt, krope_prompt, lat_sample, krope_sample,
            ssd_prompt, ssd_sample, ssd_conv_prompt, ssd_conv_sample, sc_conv_prompt, sc_conv_sample)
```

```python
import functools
import math

import jax
import jax.numpy as jnp
from jax import lax
from jax.experimental import pallas as pl
from jax.experimental.pallas import tpu as pltpu

F32 = jnp.float32
BF16 = jnp.bfloat16

SC_WIDTH = 3
SSD_HEADDIM = 64
SSD_GROUPS = 8
SSD_STATE = 128
SSD_CONV = 4
SSD_CHUNK = 128
MLA_HEADS = 16
MLA_NOPE = 128
MLA_ROPE = 64
MLA_V = 128
ROPE_THETA = 10000.0
ATT_BLOCK = 256
EPS = 1e-6

LANE = 128
VMEM_LIMIT = 56 * 1024 * 1024


def _cparams(sem, vmem=None):
    return pltpu.CompilerParams(dimension_semantics=sem, vmem_limit_bytes=vmem)


def _row_block(m, target, mult=16):
    best = None
    for d in range(mult, min(m, target) + 1, mult):
        if m % d == 0:
            best = d
    assert best is not None, (m, target)
    return best


def _sigmoid(x):
    return 1.0 / (1.0 + jnp.exp(-x))


def _silu(x):
    return x * _sigmoid(x)


def _softplus(x):
    return jnp.maximum(x, 0.0) + jnp.log1p(jnp.exp(-jnp.abs(x)))


def _dot(a, b):
    return jnp.dot(a, b, preferred_element_type=F32)


def _dot_nt(a, b):
    return lax.dot_general(a, b, (((1,), (1,)), ((), ())), preferred_element_type=F32)


def _rmsnorm_body(x_ref, w_ref, o_ref):
    x = x_ref[...].astype(F32)
    ms = jnp.mean(x * x, axis=-1, keepdims=True)
    o_ref[...] = (x * lax.rsqrt(ms + EPS) * w_ref[...]).astype(o_ref.dtype)


def rmsnorm_rows(x, w, out_dtype, *, row0=0, rows=None, br_target=512):
    m, c = x.shape
    rows = m if rows is None else rows
    br = _row_block(math.gcd(rows, row0) if row0 else rows, br_target)
    off = row0 // br
    return pl.pallas_call(
        _rmsnorm_body,
        grid=(rows // br,),
        in_specs=[pl.BlockSpec((br, c), lambda i: (i + off, 0)),
                  pl.BlockSpec((1, c), lambda i: (0, 0))],
        out_specs=pl.BlockSpec((br, c), lambda i: (i, 0)),
        out_shape=jax.ShapeDtypeStruct((rows, c), out_dtype),
        compiler_params=_cparams(("parallel",)),
        name="rmsnorm_rows",
    )(x, w.reshape(1, c))


def _mm_body(x_ref, w_ref, o_ref, *, act):
    acc = _dot(x_ref[...], w_ref[...])
    if act == "relu2":
        r = jnp.maximum(acc, 0.0)
        acc = r * r
    o_ref[...] = acc.astype(o_ref.dtype)


def matmul(x, w, out_dtype, *, act=None, bm_target=1040, bn=512):
    m, k = x.shape
    n = w.shape[1]
    bm = _row_block(m, bm_target)
    assert n % bn == 0
    return pl.pallas_call(
        functools.partial(_mm_body, act=act),
        grid=(m // bm, n // bn),
        in_specs=[pl.BlockSpec((bm, k), lambda i, j: (i, 0)),
                  pl.BlockSpec((k, bn), lambda i, j: (0, j))],
        out_specs=pl.BlockSpec((bm, bn), lambda i, j: (i, j)),
        out_shape=jax.ShapeDtypeStruct((m, n), out_dtype),
        compiler_params=_cparams(("parallel", "arbitrary"), VMEM_LIMIT),
        name="matmul",
    )(x, w)


def _mm_res_body(x_ref, w_ref, r_ref, o_ref, acc_ref, *, nk):
    kk = pl.program_id(2)

    @pl.when(kk == 0)
    def _():
        acc_ref[...] = jnp.zeros_like(acc_ref)

    acc_ref[...] += _dot(x_ref[...], w_ref[...])

    @pl.when(kk == nk - 1)
    def _():
        o_ref[...] = r_ref[...] + acc_ref[...]


def matmul_residual(x, w, res, *, bm_target=1040, bn=1024, bk=2048):
    m, k = x.shape
    n = w.shape[1]
    bm = _row_block(m, bm_target)
    bk = min(bk, k)
    assert n % bn == 0 and k % bk == 0
    nk = k // bk
    return pl.pallas_call(
        functools.partial(_mm_res_body, nk=nk),
        grid=(m // bm, n // bn, nk),
        in_specs=[pl.BlockSpec((bm, bk), lambda i, j, kk: (i, kk)),
                  pl.BlockSpec((bk, bn), lambda i, j, kk: (kk, j)),
                  pl.BlockSpec((bm, bn), lambda i, j, kk: (i, j))],
        out_specs=pl.BlockSpec((bm, bn), lambda i, j, kk: (i, j)),
        out_shape=jax.ShapeDtypeStruct((m, n), F32),
        scratch_shapes=[pltpu.VMEM((bm, bn), F32)],
        compiler_params=_cparams(("parallel", "parallel", "arbitrary"), VMEM_LIMIT),
        name="matmul_residual",
    )(x, w, res)


def _merge_body(usc_ref, ussd_ref, o_ref, wsc_ref, wssd_ref, wo_ref,
                g0_ref, g1_ref, g2_ref, gb_ref, out_ref):
    def branch(u_ref, w_ref, g_ref, k):
        y = _dot(u_ref[...], w_ref[...])
        return _sigmoid(g_ref[...] + gb_ref[k:k + 1, :]) * y

    merged = branch(usc_ref, wsc_ref, g0_ref, 0) + branch(ussd_ref, wssd_ref, g1_ref, 1)
    merged = merged + branch(o_ref, wo_ref, g2_ref, 2)
    out_ref[...] = merged.astype(out_ref.dtype)


def branch_merge(u_sc, u_ssd, o_mla, w_sc, w_ssd, w_o, proj, gate_col0, gate_b,
                 *, bm_target=1040, bn=256):
    m, kdim = u_sc.shape
    d = w_sc.shape[1]
    bm = _row_block(m, bm_target)
    goff = gate_col0 // bn
    nd = d // bn
    u_spec = pl.BlockSpec((bm, kdim), lambda i, j: (i, 0))
    w_spec = pl.BlockSpec((kdim, bn), lambda i, j: (0, j))

    def g_spec(k):
        return pl.BlockSpec((bm, bn), lambda i, j: (i, goff + k * nd + j))

    return pl.pallas_call(
        _merge_body,
        grid=(m // bm, nd),
        in_specs=[u_spec, u_spec, u_spec, w_spec, w_spec, w_spec,
                  g_spec(0), g_spec(1), g_spec(2),
                  pl.BlockSpec((3, bn), lambda i, j: (0, j))],
        out_specs=pl.BlockSpec((bm, bn), lambda i, j: (i, j)),
        out_shape=jax.ShapeDtypeStruct((m, d), BF16),
        compiler_params=_cparams(("parallel", "arbitrary"), VMEM_LIMIT),
        name="branch_merge",
    )(u_sc, u_ssd, o_mla, w_sc, w_ssd, w_o, proj, proj, proj, gate_b)


def _head_mm_body(x_ref, w_ref, o_ref):
    o_ref[...] = _dot(x_ref[...].astype(BF16), w_ref[...]).astype(o_ref.dtype)


def q_latent(q, w_uk_t, *, bm_target=1040):
    m = q.shape[0]
    h, nope, r = w_uk_t.shape
    bm = _row_block(m, bm_target)
    return pl.pallas_call(
        _head_mm_body,
        grid=(m // bm, h),
        in_specs=[pl.BlockSpec((bm, nope), lambda i, hh: (i, hh)),
                  pl.BlockSpec((None, nope, r), lambda i, hh: (hh, 0, 0))],
        out_specs=pl.BlockSpec((None, bm, r), lambda i, hh: (hh, i, 0)),
        out_shape=jax.ShapeDtypeStruct((h, m, r), BF16),
        compiler_params=_cparams(("parallel", "arbitrary")),
        name="q_latent",
    )(q, w_uk_t)


def v_up(o_lat, w_uv_h, *, bm_target=1040):
    h, m, r = o_lat.shape
    v = w_uv_h.shape[2]
    bm = _row_block(m, bm_target)
    return pl.pallas_call(
        _head_mm_body,
        grid=(m // bm, h),
        in_specs=[pl.BlockSpec((None, bm, r), lambda i, hh: (hh, i, 0)),
                  pl.BlockSpec((None, r, v), lambda i, hh: (hh, 0, 0))],
        out_specs=pl.BlockSpec((bm, v), lambda i, hh: (i, hh)),
        out_shape=jax.ShapeDtypeStruct((m, h * v), BF16),
        compiler_params=_cparams(("parallel", "arbitrary")),
        name="v_up",
    )(o_lat, w_uv_h)


def _shift_rows(u, k, row):
    return jnp.where(row >= k, pltpu.roll(u, k, 0), 0.0)


def _sc_conv_prompt_body(b_ref, c_ref, x_ref, w_ref, u_ref, tail_ref):
    u = c_ref[...] * x_ref[...]
    seq = u.shape[0]
    row = lax.broadcasted_iota(jnp.int32, u.shape, 0)
    v = _shift_rows(u, 2, row) * w_ref[0:1, :]
    v = v + _shift_rows(u, 1, row) * w_ref[1:2, :]
    v = v + u * w_ref[2:3, :]
    u_ref[...] = (b_ref[...] * v).astype(u_ref.dtype)
    tail_ref[...] = u[seq - (SC_WIDTH - 1):, :]


def sc_conv_prompt(proj, w, batch, seq, *, bc=512):
    sc_dim = w.shape[1]
    nc = sc_dim // bc
    return pl.pallas_call(
        _sc_conv_prompt_body,
        grid=(batch, nc),
        in_specs=[pl.BlockSpec((seq, bc), lambda b, j: (b, j)),
                  pl.BlockSpec((seq, bc), lambda b, j: (b, nc + j)),
                  pl.BlockSpec((seq, bc), lambda b, j: (b, 2 * nc + j)),
                  pl.BlockSpec((SC_WIDTH, bc), lambda b, j: (0, j))],
        out_specs=[pl.BlockSpec((seq, bc), lambda b, j: (b, j)),
                   pl.BlockSpec((None, SC_WIDTH - 1, bc), lambda b, j: (b, 0, j))],
        out_shape=[jax.ShapeDtypeStruct((batch * seq, sc_dim), BF16),
                   jax.ShapeDtypeStruct((batch, SC_WIDTH - 1, sc_dim), F32)],
        compiler_params=_cparams(("parallel", "parallel"), VMEM_LIMIT),
        name="sc_conv_prompt",
    )(proj, proj, proj, w)


def _sc_conv_sample_body(b_ref, c_ref, x_ref, buf_ref, w_ref, u_ref, nbuf_ref):
    u = c_ref[...] * x_ref[...]
    v = buf_ref[0] * w_ref[0:1, :]
    v = v + buf_ref[1] * w_ref[1:2, :]
    v = v + u * w_ref[2:3, :]
    u_ref[...] = (b_ref[...] * v).astype(u_ref.dtype)
    nbuf_ref[0] = buf_ref[1]
    nbuf_ref[1] = u


def sc_conv_sample(proj, buf_t, w, row_block, nb):
    sc_dim = w.shape[1]
    return pl.pallas_call(
        _sc_conv_sample_body,
        grid=(1,),
        in_specs=[pl.BlockSpec((nb, sc_dim), lambda i: (row_block, 0)),
                  pl.BlockSpec((nb, sc_dim), lambda i: (row_block, 1)),
                  pl.BlockSpec((nb, sc_dim), lambda i: (row_block, 2)),
                  pl.BlockSpec((SC_WIDTH - 1, nb, sc_dim), lambda i: (0, 0, 0)),
                  pl.BlockSpec((SC_WIDTH, sc_dim), lambda i: (0, 0))],
        out_specs=[pl.BlockSpec((nb, sc_dim), lambda i: (0, 0)),
                   pl.BlockSpec((SC_WIDTH - 1, nb, sc_dim), lambda i: (0, 0, 0))],
        out_shape=[jax.ShapeDtypeStruct((nb, sc_dim), BF16),
                   jax.ShapeDtypeStruct((SC_WIDTH - 1, nb, sc_dim), F32)],
        compiler_params=_cparams(("arbitrary",)),
        name="sc_conv_sample",
    )(proj, proj, proj, buf_t, w)


def _ssd_conv_prompt_body(x_ref, w_ref, bias_ref, o_ref, tail_ref):
    u = x_ref[...]
    seq = u.shape[0]
    row = lax.broadcasted_iota(jnp.int32, u.shape, 0)
    v = _shift_rows(u, 3, row) * w_ref[0:1, :]
    v = v + _shift_rows(u, 2, row) * w_ref[1:2, :]
    v = v + _shift_rows(u, 1, row) * w_ref[2:3, :]
    v = v + u * w_ref[3:4, :]
    o_ref[...] = _silu(v + bias_ref[...])
    tail_ref[...] = u[seq - (SSD_CONV - 1):, :]


def ssd_conv_prompt(proj, col0, w, bias, batch, seq, *, bc=512):
    cdim = w.shape[1]
    nc = cdim // bc
    off = col0 // bc
    return pl.pallas_call(
        _ssd_conv_prompt_body,
        grid=(batch, nc),
        in_specs=[pl.BlockSpec((seq, bc), lambda b, j: (b, off + j)),
                  pl.BlockSpec((SSD_CONV, bc), lambda b, j: (0, j)),
                  pl.BlockSpec((1, bc), lambda b, j: (0, j))],
        out_specs=[pl.BlockSpec((seq, bc), lambda b, j: (b, j)),
                   pl.BlockSpec((None, SSD_CONV - 1, bc), lambda b, j: (b, 0, j))],
        out_shape=[jax.ShapeDtypeStruct((batch * seq, cdim), F32),
                   jax.ShapeDtypeStruct((batch, SSD_CONV - 1, cdim), F32)],
        compiler_params=_cparams(("parallel", "parallel"), VMEM_LIMIT),
        name="ssd_conv_prompt",
    )(proj, w, bias.reshape(1, cdim))


def _cumsum_rows(a):
    n = a.shape[0]
    row = lax.broadcasted_iota(jnp.int32, a.shape, 0)
    k = 1
    while k < n:
        a = a + _shift_rows(a, k, row)
        k *= 2
    return a


def _ssd_prompt_body(xs_ref, b_ref, c_ref, z_ref, dt_ref, dtb_ref, alog_ref, d_ref, nw_ref,
                     u_ref, st_ref, h_ref, y_ref, *, n_chunks):
    g = pl.program_id(1)
    c = pl.program_id(2)
    e_per_g = y_ref.shape[1] // SSD_HEADDIM
    q = xs_ref.shape[0]

    @pl.when(c == 0)
    def _():
        h_ref[...] = jnp.zeros_like(h_ref)

    x = xs_ref[...]
    bmat = b_ref[...].astype(BF16)
    cmat = c_ref[...].astype(BF16)
    dt_all = _softplus(dt_ref[...] + dtb_ref[...])
    acum_all = _cumsum_rows(dt_all * (-jnp.exp(alog_ref[...])))
    shift = (LANE - e_per_g * g) % LANE
    dt_g = pltpu.roll(dt_all, shift, 1)
    ac_g = pltpu.roll(acum_all, shift, 1)
    dt_t = dt_g.T
    ac_t = ac_g.T
    x_t = x.T
    cb = _dot_nt(cmat, bmat)
    tri = (lax.broadcasted_iota(jnp.int32, (q, q), 0)
           >= lax.broadcasted_iota(jnp.int32, (q, q), 1))
    for e in range(e_per_g):
        lo, hi = e * SSD_HEADDIM, (e + 1) * SSD_HEADDIM
        a_col = ac_g[:, e:e + 1]
        a_row = ac_t[e:e + 1, :]
        dt_row = dt_t[e:e + 1, :]
        a_last = ac_g[q - 1:q, e:e + 1]
        decay = jnp.exp(jnp.where(tri, a_col - a_row, -jnp.inf))
        w_ts = cb * decay * dt_row
        xe = x[:, lo:hi]
        he = h_ref[lo:hi, :]
        y = _dot(w_ts.astype(BF16), xe.astype(BF16))
        y = y + _dot_nt(cmat, he.astype(BF16)) * jnp.exp(a_col)
        y_ref[:, lo:hi] = y + xe * d_ref[:, lo:hi]
        to_end = jnp.exp(a_last - a_row) * dt_row
        s_chunk = _dot((x_t[lo:hi, :] * to_end).astype(BF16), bmat)
        h_ref[lo:hi, :] = he * jnp.exp(a_last) + s_chunk

    yz = y_ref[...] * _silu(z_ref[...])
    ms = jnp.mean(yz * yz, axis=-1, keepdims=True)
    u_ref[...] = (yz * lax.rsqrt(ms + EPS) * nw_ref[...]).astype(u_ref.dtype)

    @pl.when(c == n_chunks - 1)
    def _():
        st_ref[...] = h_ref[...]


def ssd_prompt(xbc, proj, z_col0, dt_col0, dt_bias, a_log, d_skip, norm_w, batch, seq):
    ssd_dim = norm_w.shape[0]
    gw = ssd_dim // SSD_GROUPS
    q = SSD_CHUNK
    n_chunks = seq // q
    nb0 = ssd_dim // SSD_STATE
    pad = lambda v: jnp.zeros((1, LANE), F32).at[0, :v.shape[0]].set(v)
    row = lambda b, c: b * n_chunks + c
    return pl.pallas_call(
        functools.partial(_ssd_prompt_body, n_chunks=n_chunks),
        grid=(batch, SSD_GROUPS, n_chunks),
        in_specs=[pl.BlockSpec((q, gw), lambda b, g, c: (row(b, c), g)),
                  pl.BlockSpec((q, SSD_STATE), lambda b, g, c: (row(b, c), nb0 + g)),
                  pl.BlockSpec((q, SSD_STATE), lambda b, g, c: (row(b, c), nb0 + SSD_GROUPS + g)),
                  pl.BlockSpec((q, gw), lambda b, g, c: (row(b, c), z_col0 // gw + g)),
                  pl.BlockSpec((q, LANE), lambda b, g, c: (row(b, c), dt_col0 // LANE)),
                  pl.BlockSpec((1, LANE), lambda b, g, c: (0, 0)),
                  pl.BlockSpec((1, LANE), lambda b, g, c: (0, 0)),
                  pl.BlockSpec((1, gw), lambda b, g, c: (0, g)),
                  pl.BlockSpec((1, gw), lambda b, g, c: (0, g))],
        out_specs=[pl.BlockSpec((q, gw), lambda b, g, c: (row(b, c), g)),
                   pl.BlockSpec((None, None, gw, SSD_STATE), lambda b, g, c: (b, g, 0, 0))],
        out_shape=[jax.ShapeDtypeStruct((batch * seq, ssd_dim), BF16),
                   jax.ShapeDtypeStruct((batch, SSD_GROUPS, gw, SSD_STATE), F32)],
        scratch_shapes=[pltpu.VMEM((gw, SSD_STATE), F32), pltpu.VMEM((q, gw), F32)],
        compiler_params=_cparams(("parallel", "parallel", "arbitrary")),
        name="ssd_prompt",
    )(xbc, xbc, xbc, proj, proj, pad(dt_bias), pad(a_log),
      jnp.repeat(d_skip, SSD_HEADDIM).reshape(1, ssd_dim), norm_w.reshape(1, ssd_dim))


def _ssd_sample_pre_body(x_ref, buf_ref, w_ref, bias_ref, dt_ref, dtb_ref, alog_ref,
                         act_ref, nbuf_ref, da_t_ref, dtx_t_ref, *, ssd_dim, n_heads):
    u = x_ref[...]
    v = buf_ref[0] * w_ref[0:1, :]
    v = v + buf_ref[1] * w_ref[1:2, :]
    v = v + buf_ref[2] * w_ref[2:3, :]
    v = v + u * w_ref[3:4, :]
    act = _silu(v + bias_ref[...])
    act_ref[...] = act
    nbuf_ref[0] = buf_ref[1]
    nbuf_ref[1] = buf_ref[2]
    nbuf_ref[2] = u
    dt = _softplus(dt_ref[...] + dtb_ref[...])
    da = jnp.exp(dt * (-jnp.exp(alog_ref[...])))
    dt_t = dt.T
    da_t = da.T
    x_t = act[:, :ssd_dim].T
    nb = u.shape[0]
    for h in range(n_heads):
        lo, hi = h * SSD_HEADDIM, (h + 1) * SSD_HEADDIM
        da_t_ref[lo:hi, :] = jnp.broadcast_to(da_t[h:h + 1, :], (SSD_HEADDIM, nb))
        dtx_t_ref[lo:hi, :] = dt_t[h:h + 1, :] * x_t[lo:hi, :]


def ssd_sample_pre(proj, xbc_col0, dt_col0, buf_t, w, bias, dt_bias, a_log, row_block, nb, ssd_dim):
    cdim = w.shape[1]
    n_heads = ssd_dim // SSD_HEADDIM
    pad = lambda v: jnp.zeros((1, LANE), F32).at[0, :v.shape[0]].set(v)
    full = lambda *s: pl.BlockSpec(s, lambda i: (0,) * len(s))
    return pl.pallas_call(
        functools.partial(_ssd_sample_pre_body, ssd_dim=ssd_dim, n_heads=n_heads),
        grid=(1,),
        in_specs=[pl.BlockSpec((nb, cdim), lambda i: (row_block, xbc_col0 // cdim)),
                  full(SSD_CONV - 1, nb, cdim), full(SSD_CONV, cdim), full(1, cdim),
                  pl.BlockSpec((nb, LANE), lambda i: (row_block, dt_col0 // LANE)),
                  full(1, LANE), full(1, LANE)],
        out_specs=[full(nb, cdim), full(SSD_CONV - 1, nb, cdim), full(ssd_dim, nb), full(ssd_dim, nb)],
        out_shape=[jax.ShapeDtypeStruct((nb, cdim), F32),
                   jax.ShapeDtypeStruct((SSD_CONV - 1, nb, cdim), F32),
                   jax.ShapeDtypeStruct((ssd_dim, nb), F32),
                   jax.ShapeDtypeStruct((ssd_dim, nb), F32)],
        compiler_params=_cparams(("arbitrary",), VMEM_LIMIT),
        name="ssd_sample_pre",
    )(proj, buf_t, w, bias.reshape(1, cdim), proj, pad(dt_bias), pad(a_log))


def _ssd_sample_step_body(st_ref, da_t_ref, dtx_t_ref, b_ref, c_ref, nst_ref, y_t_ref):
    b = pl.program_id(0)

    @pl.when(b == 0)
    def _():
        y_t_ref[...] = jnp.zeros_like(y_t_ref)

    rows, nb = da_t_ref.shape
    groups, n = b_ref.shape
    lane = lax.broadcasted_iota(jnp.int32, (rows, nb), 1)
    pick = lambda t: jnp.sum(jnp.where(lane == b, t, 0.0), axis=1, keepdims=True)
    da = pick(da_t_ref[...])
    dtx = pick(dtx_t_ref[...])
    rep = lambda m: jnp.broadcast_to(m[:, None, :], (groups, rows // groups, n)).reshape(rows, n)
    h_new = st_ref[...] * da + dtx * rep(b_ref[...])
    nst_ref[...] = h_new
    y = jnp.sum(h_new * rep(c_ref[...]), axis=1, keepdims=True)
    y_t_ref[...] = jnp.where(lane == b, y, y_t_ref[...])


def ssd_sample_step(state, da_t, dtx_t, bmat, cmat):
    nb, rows, n = state.shape
    groups = bmat.shape[1]
    return pl.pallas_call(
        _ssd_sample_step_body,
        grid=(nb,),
        in_specs=[pl.BlockSpec((None, rows, n), lambda b: (b, 0, 0)),
                  pl.BlockSpec((rows, nb), lambda b: (0, 0)),
                  pl.BlockSpec((rows, nb), lambda b: (0, 0)),
                  pl.BlockSpec((None, groups, n), lambda b: (b, 0, 0)),
                  pl.BlockSpec((None, groups, n), lambda b: (b, 0, 0))],
        out_specs=[pl.BlockSpec((None, rows, n), lambda b: (b, 0, 0)),
                   pl.BlockSpec((rows, nb), lambda b: (0, 0))],
        out_shape=[jax.ShapeDtypeStruct((nb, rows, n), F32),
                   jax.ShapeDtypeStruct((rows, nb), F32)],
        compiler_params=_cparams(("arbitrary",)),
        name="ssd_sample_step",
    )(state, da_t, dtx_t, bmat, cmat)


def _ssd_sample_post_body(y_t_ref, x_ref, z_ref, d_ref, nw_ref, u_ref, *, gw):
    y = y_t_ref[...].T + x_ref[...] * d_ref[...]
    yz = y * _silu(z_ref[...])
    for g in range(yz.shape[1] // gw):
        blk = yz[:, g * gw:(g + 1) * gw]
        ms = jnp.mean(blk * blk, axis=-1, keepdims=True)
        u_ref[:, g * gw:(g + 1) * gw] = (
            blk * lax.rsqrt(ms + EPS) * nw_ref[:, g * gw:(g + 1) * gw]).astype(u_ref.dtype)


def ssd_sample_post(y_t, act, proj, z_col0, d_rep, norm_w, row_block, nb):
    ssd_dim = norm_w.shape[0]
    gw = ssd_dim // SSD_GROUPS
    full = lambda *s: pl.BlockSpec(s, lambda i: (0,) * len(s))
    return pl.pallas_call(
        functools.partial(_ssd_sample_post_body, gw=gw),
        grid=(1,),
        in_specs=[full(ssd_dim, nb),
                  pl.BlockSpec((nb, ssd_dim), lambda i: (0, 0)),
                  pl.BlockSpec((nb, ssd_dim), lambda i: (row_block, z_col0 // ssd_dim)),
                  full(1, ssd_dim), full(1, ssd_dim)],
        out_specs=full(nb, ssd_dim),
        out_shape=jax.ShapeDtypeStruct((nb, ssd_dim), BF16),
        compiler_params=_cparams(("arbitrary",)),
        name="ssd_sample_post",
    )(y_t, act, proj, d_rep.reshape(1, ssd_dim), norm_w.reshape(1, ssd_dim))


def _rope_pairs(x, cos, sin_signed, half, lane):
    width = x.shape[1]
    first = (lane % (2 * half)) < half
    partner = jnp.where(first, pltpu.roll(x, width - half, 1), pltpu.roll(x, half, 1))
    return x * cos + partner * sin_signed


def _mla_pre_body(p_ref, qw_ref, kvw_ref, cos_ref, sin_ref,
                  cqn_ref, lat_ref, latb_ref, kr_ref, krb_ref, *, q_lora, kv_lora):
    cq = p_ref[:, :q_lora]
    ms = jnp.mean(cq * cq, axis=-1, keepdims=True)
    cqn_ref[...] = (cq * lax.rsqrt(ms + EPS) * qw_ref[...]).astype(cqn_ref.dtype)
    ckv = p_ref[:, q_lora:q_lora + kv_lora]
    ms = jnp.mean(ckv * ckv, axis=-1, keepdims=True)
    lat = ckv * lax.rsqrt(ms + EPS) * kvw_ref[...]
    lat_ref[...] = lat
    latb_ref[...] = lat.astype(BF16)
    kr = p_ref[:, q_lora + kv_lora:q_lora + kv_lora + LANE]
    lane = lax.broadcasted_iota(jnp.int32, kr.shape, 1)
    rot = _rope_pairs(kr, cos_ref[...], sin_ref[...], MLA_ROPE // 2, lane)[:, :MLA_ROPE]
    kr_ref[...] = rot
    krb_ref[...] = rot.astype(BF16)


def mla_pre(proj, col0, width, q_norm_w, kv_norm_w, cos_k, sin_k, tbl_index, *, br=128):
    m = proj.shape[0]
    q_lora, kv_lora = q_norm_w.shape[0], kv_norm_w.shape[0]
    row = lambda i: (i, 0)
    return pl.pallas_call(
        functools.partial(_mla_pre_body, q_lora=q_lora, kv_lora=kv_lora),
        grid=(m // br,),
        in_specs=[pl.BlockSpec((br, width), lambda i: (i, col0 // width)),
                  pl.BlockSpec((1, q_lora), lambda i: (0, 0)),
                  pl.BlockSpec((1, kv_lora), lambda i: (0, 0)),
                  pl.BlockSpec((br, LANE), lambda i: (tbl_index(i), 0)),
                  pl.BlockSpec((br, LANE), lambda i: (tbl_index(i), 0))],
        out_specs=[pl.BlockSpec((br, q_lora), row), pl.BlockSpec((br, kv_lora), row),
                   pl.BlockSpec((br, kv_lora), row), pl.BlockSpec((br, MLA_ROPE), row),
                   pl.BlockSpec((br, MLA_ROPE), row)],
        out_shape=[jax.ShapeDtypeStruct((m, q_lora), BF16),
                   jax.ShapeDtypeStruct((m, kv_lora), F32),
                   jax.ShapeDtypeStruct((m, kv_lora), BF16),
                   jax.ShapeDtypeStruct((m, MLA_ROPE), F32),
                   jax.ShapeDtypeStruct((m, MLA_ROPE), BF16)],
        compiler_params=_cparams(("parallel",)),
        name="mla_pre",
    )(proj, q_norm_w.reshape(1, q_lora), kv_norm_w.reshape(1, kv_lora), cos_k, sin_k)


def _q_rope_body(q_ref, cos_ref, sin_ref, o_ref):
    x = q_ref[...]
    lane = lax.broadcasted_iota(jnp.int32, x.shape, 1)
    rot = _rope_pairs(x, cos_ref[...], sin_ref[...], MLA_ROPE // 2, lane)
    for h in range(o_ref.shape[0]):
        o_ref[h] = rot[:, h * MLA_ROPE:(h + 1) * MLA_ROPE].astype(o_ref.dtype)


def q_rope(q, col0, cos_q, sin_q, tbl_index, *, br=128):
    m = q.shape[0]
    width = MLA_HEADS * MLA_ROPE
    return pl.pallas_call(
        _q_rope_body,
        grid=(m // br,),
        in_specs=[pl.BlockSpec((br, width), lambda i: (i, col0 // width)),
                  pl.BlockSpec((br, width), lambda i: (tbl_index(i), 0)),
                  pl.BlockSpec((br, width), lambda i: (tbl_index(i), 0))],
        out_specs=pl.BlockSpec((MLA_HEADS, br, MLA_ROPE), lambda i: (0, i, 0)),
        out_shape=jax.ShapeDtypeStruct((MLA_HEADS, m, MLA_ROPE), BF16),
        compiler_params=_cparams(("parallel",)),
        name="q_rope",
    )(q, cos_q, sin_q)


def _attn_prompt_body(ql_ref, qr_ref, lat_ref, kr_ref, o_ref, m_ref, l_ref, acc_ref, *, scale):
    qi = pl.program_id(1)
    h, tq, r = ql_ref.shape
    rows = h * tq
    ql = ql_ref[...].reshape(rows, r)
    qr = qr_ref[...].reshape(rows, qr_ref.shape[2])
    m_ref[...] = jnp.full_like(m_ref, -jnp.inf)
    l_ref[...] = jnp.zeros_like(l_ref)
    acc_ref[...] = jnp.zeros_like(acc_ref)

    def update(j, masked):
        k0 = pl.multiple_of(j * tq, tq)
        kl = lat_ref[pl.ds(k0, tq), :]
        kr = kr_ref[pl.ds(k0, tq), :]
        s = (_dot_nt(ql, kl) + _dot_nt(qr, kr)) * scale
        if masked:
            s3 = s.reshape(h, tq, tq)
            ok = (lax.broadcasted_iota(jnp.int32, s3.shape, 2)
                  <= lax.broadcasted_iota(jnp.int32, s3.shape, 1))
            s = jnp.where(ok, s3, -jnp.inf).reshape(rows, tq)
        m_old = m_ref[...]
        m_new = jnp.maximum(m_old, jnp.max(s, axis=-1, keepdims=True))
        alpha = jnp.exp(m_old - m_new)
        p = jnp.exp(s - m_new)
        l_ref[...] = alpha * l_ref[...] + jnp.sum(p, axis=-1, keepdims=True)
        acc_ref[...] = alpha * acc_ref[...] + _dot(p.astype(BF16), kl)
        m_ref[...] = m_new

    def body(j, carry):
        update(j, False)
        return carry

    lax.fori_loop(0, qi, body, 0)
    update(qi, True)
    o_ref[...] = (acc_ref[...] / l_ref[...]).reshape(h, tq, r).astype(o_ref.dtype)


def attn_prompt(ql, qr, latb, krb, batch, seq, scale, *, tq=ATT_BLOCK):
    h, _, r = ql.shape
    rope = qr.shape[2]
    nq = seq // tq
    rows = h * tq
    return pl.pallas_call(
        functools.partial(_attn_prompt_body, scale=scale),
        grid=(batch, nq),
        in_specs=[pl.BlockSpec((h, tq, r), lambda b, i: (0, b * nq + i, 0)),
                  pl.BlockSpec((h, tq, rope), lambda b, i: (0, b * nq + i, 0)),
                  pl.BlockSpec((seq, r), lambda b, i: (b, 0)),
                  pl.BlockSpec((seq, rope), lambda b, i: (b, 0))],
        out_specs=pl.BlockSpec((h, tq, r), lambda b, i: (0, b * nq + i, 0)),
        out_shape=jax.ShapeDtypeStruct((h, batch * seq, r), BF16),
        scratch_shapes=[pltpu.VMEM((rows, 1), F32), pltpu.VMEM((rows, 1), F32),
                        pltpu.VMEM((rows, r), F32)],
        compiler_params=_cparams(("parallel", "parallel"), VMEM_LIMIT),
        name="attn_prompt",
    )(ql, qr, latb, krb)


def _attn_sample_body(pt_ref, ql_ref, qr_ref, latn_ref, krn_ref, clat_ref, ckr_ref, o_ref,
                      lbuf, kbuf, lsem, ksem, *, layer, scale, n_pages, page):
    b = pl.program_id(0)
    nb = pl.num_programs(0)

    def copies(bb, slot):
        out = []
        for p in range(n_pages):
            pg = pt_ref[bb, p]
            out.append(pltpu.make_async_copy(
                clat_ref.at[pg, :, layer, :], lbuf.at[slot, pl.ds(p * page, page), :], lsem.at[slot]))
            out.append(pltpu.make_async_copy(
                ckr_ref.at[pg, :, layer, :], kbuf.at[slot, pl.ds(p * page, page), :], ksem.at[slot]))
        return out

    @pl.when(b == 0)
    def _():
        for cp in copies(0, 0):
            cp.start()

    slot = b % 2

    @pl.when(b + 1 < nb)
    def _():
        for cp in copies(b + 1, 1 - slot):
            cp.start()

    for cp in copies(b, slot):
        cp.wait()

    ql = ql_ref[...]
    qr = qr_ref[...]
    kl = lbuf[slot].astype(BF16)
    kr = kbuf[slot].astype(BF16)
    ln = latn_ref[...]
    kn = krn_ref[...]
    lnb = ln.astype(BF16).astype(F32)
    knb = kn.astype(BF16).astype(F32)
    s_past = (_dot_nt(ql, kl) + _dot_nt(qr, kr)) * scale
    s_new = (jnp.sum(ql.astype(F32) * lnb, axis=-1, keepdims=True)
             + jnp.sum(qr.astype(F32) * knb, axis=-1, keepdims=True)) * scale
    m = jnp.maximum(jnp.max(s_past, axis=-1, keepdims=True), s_new)
    p_past = jnp.exp(s_past - m)
    p_new = jnp.exp(s_new - m)
    den = jnp.sum(p_past, axis=-1, keepdims=True) + p_new
    num = _dot(p_past.astype(BF16), kl) + p_new.astype(BF16).astype(F32) * lnb
    o_ref[...] = (num / den).astype(o_ref.dtype)


def attn_sample(page_table, ql_s, qr_s, lat_new, kr_new, cache_lat, cache_kr, layer, scale):
    nb, h, r = ql_s.shape
    rope = qr_s.shape[2]
    n_pages = page_table.shape[1]
    page = cache_lat.shape[1]
    t = n_pages * page
    grid_spec = pltpu.PrefetchScalarGridSpec(
        num_scalar_prefetch=1,
        grid=(nb,),
        in_specs=[pl.BlockSpec((None, h, r), lambda b, pt: (b, 0, 0)),
                  pl.BlockSpec((None, h, rope), lambda b, pt: (b, 0, 0)),
                  pl.BlockSpec((None, 1, r), lambda b, pt: (b, 0, 0)),
                  pl.BlockSpec((None, 1, rope), lambda b, pt: (b, 0, 0)),
                  pl.BlockSpec(memory_space=pl.ANY),
                  pl.BlockSpec(memory_space=pl.ANY)],
        out_specs=pl.BlockSpec((None, h, r), lambda b, pt: (b, 0, 0)),
        scratch_shapes=[pltpu.VMEM((2, t, r), F32), pltpu.VMEM((2, t, rope), F32),
                        pltpu.SemaphoreType.DMA((2,)), pltpu.SemaphoreType.DMA((2,))],
    )
    return pl.pallas_call(
        functools.partial(_attn_sample_body, layer=layer, scale=scale, n_pages=n_pages, page=page),
        grid_spec=grid_spec,
        out_shape=jax.ShapeDtypeStruct((nb, h, r), BF16),
        compiler_params=_cparams(("arbitrary",), VMEM_LIMIT),
        name="attn_sample",
    )(page_table, ql_s, qr_s, lat_new, kr_new, cache_lat, cache_kr)


def _rope_tables(positions, reps):
    half = MLA_ROPE // 2
    inv = ROPE_THETA ** (-jnp.arange(half, dtype=F32) / half)
    ang = positions.astype(F32)[:, None] * inv[None, :]
    cos = jnp.cos(ang)
    sin = jnp.sin(ang)
    cos = jnp.concatenate([cos, cos], axis=1)
    sin = jnp.concatenate([-sin, sin], axis=1)
    return jnp.tile(cos, (1, reps)), jnp.tile(sin, (1, reps))


def kernel(x_prompt, x_sample, cache_mla_latent, cache_mla_krope, state_ssd, state_ssd_conv, state_short_conv, page_table, norm_mix_w, w_in, gate_b, sc_conv_w, sc_w_out, ssd_conv_w, ssd_conv_b, ssd_dt_bias, ssd_A_log, ssd_D, ssd_norm_w, ssd_w_out, mla_q_norm_w, mla_w_q_up, mla_kv_norm_w, mla_w_uk, mla_w_uv, mla_w_o, w_out, norm_mlp_w, mlp_w1, mlp_w2, final_norm_w):
    bp, sp, d = x_prompt.shape
    bs, ss, _ = x_sample.shape
    assert ss == 1
    depth = w_in.shape[0]
    mp = bp * sp
    m = mp + bs
    n_pages = page_table.shape[1]
    page = cache_mla_latent.shape[1]
    past_len = n_pages * page

    sc_dim = sc_conv_w.shape[2]
    ssd_dim = ssd_norm_w.shape[1]
    conv_dim = ssd_conv_w.shape[2]
    n_heads = ssd_dt_bias.shape[1]
    q_lora = mla_q_norm_w.shape[1]
    kv_lora = mla_kv_norm_w.shape[1]
    scale = (MLA_NOPE + MLA_ROPE) ** -0.5

    sizes = (sc_dim, sc_dim, sc_dim, ssd_dim, conv_dim, n_heads, q_lora, kv_lora, MLA_ROPE, 3 * d)
    starts = [0]
    for s in sizes:
        starts.append(starts[-1] + s)
    (o_scb, o_scc, o_scx, o_z, o_xbc, o_dt, o_cq, o_ckv, o_kr, o_gate, o_end) = starts
    c_z = 3 * sc_dim
    c_xbc = c_z + ssd_dim
    c_gate = c_xbc + conv_dim
    c_small = c_gate + 3 * d
    small_w = q_lora + kv_lora + 2 * LANE
    c_dt = c_small + q_lora + kv_lora + LANE
    n_proj = c_small + small_w
    assert n_proj % 512 == 0 and c_small % small_w == 0

    br = 128
    assert sp % br == 0 and bs <= br and mp % br == 0
    positions = jnp.concatenate([jnp.arange(sp), jnp.full((br,), past_len)])
    cos_q, sin_q = _rope_tables(positions, MLA_HEADS)
    cos_k, sin_k = _rope_tables(positions, 1)
    zpad = jnp.zeros((positions.shape[0], LANE - MLA_ROPE), F32)
    cos_k = jnp.concatenate([cos_k, zpad], axis=1)
    sin_k = jnp.concatenate([sin_k, zpad], axis=1)
    n_pblk = mp // br
    tbl_index = lambda i: jnp.where(i < n_pblk, i % (sp // br), sp // br)

    assert bs == br
    sample_blk = mp // bs
    x = jnp.concatenate([x_prompt.reshape(mp, d), x_sample.reshape(bs, d)], axis=0)

    lat_l, kr_l, sc_p, sc_s, cv_p, cv_s, st_p, st_s = [], [], [], [], [], [], [], []
    for l in range(depth):
        wl = w_in[l]
        w_proj = jnp.concatenate(
            [wl[:, :o_dt], wl[:, o_gate:], wl[:, o_cq:o_gate],
             jnp.zeros((d, LANE - MLA_ROPE), F32), wl[:, o_dt:o_cq],
             jnp.zeros((d, LANE - n_heads), F32)], axis=1).astype(BF16)
        wq = mla_w_q_up[l].reshape(q_lora, MLA_HEADS, MLA_NOPE + MLA_ROPE)
        wq = jnp.concatenate([wq[:, :, :MLA_NOPE].reshape(q_lora, -1),
                              wq[:, :, MLA_NOPE:].reshape(q_lora, -1)], axis=1).astype(BF16)
        w_uk_t = jnp.transpose(mla_w_uk[l], (1, 2, 0)).astype(BF16)
        w_uv_h = jnp.transpose(mla_w_uv[l], (1, 0, 2)).astype(BF16)

        h = rmsnorm_rows(x, norm_mix_w[l], BF16)
        proj = matmul(h, w_proj, F32)

        u_sc_p, sc_tail = sc_conv_prompt(proj, sc_conv_w[l], bp, sp)
        u_sc_s, sc_nbuf = sc_conv_sample(proj, jnp.swapaxes(state_short_conv[l], 0, 1),
                                         sc_conv_w[l], sample_blk, bs)
        u_sc = jnp.concatenate([u_sc_p, u_sc_s], axis=0)
        sc_p.append(sc_tail)
        sc_s.append(jnp.swapaxes(sc_nbuf, 0, 1))

        xbc_p, cv_tail = ssd_conv_prompt(proj, c_xbc, ssd_conv_w[l], ssd_conv_b[l], bp, sp)
        u_ssd_p, st_new_p = ssd_prompt(xbc_p, proj, c_z, c_dt, ssd_dt_bias[l], ssd_A_log[l],
                                       ssd_D[l], ssd_norm_w[l], bp, sp)
        act_s, cv_nbuf, da_t, dtx_t = ssd_sample_pre(
            proj, c_xbc, c_dt, jnp.swapaxes(state_ssd_conv[l], 0, 1), ssd_conv_w[l],
            ssd_conv_b[l], ssd_dt_bias[l], ssd_A_log[l], sample_blk, bs, ssd_dim)
        gn = SSD_GROUPS * SSD_STATE
        st_new_s, y_t = ssd_sample_step(
            state_ssd[l].reshape(bs, ssd_dim, SSD_STATE), da_t, dtx_t,
            act_s[:, ssd_dim:ssd_dim + gn].reshape(bs, SSD_GROUPS, SSD_STATE),
            act_s[:, ssd_dim + gn:].reshape(bs, SSD_GROUPS, SSD_STATE))
        u_ssd_s = ssd_sample_post(y_t, act_s, proj, c_z, jnp.repeat(ssd_D[l], SSD_HEADDIM),
                                  ssd_norm_w[l], sample_blk, bs)
        u_ssd = jnp.concatenate([u_ssd_p, u_ssd_s], axis=0)
        cv_p.append(cv_tail)
        cv_s.append(jnp.swapaxes(cv_nbuf, 0, 1))
        st_p.append(st_new_p.reshape(bp, n_heads, SSD_HEADDIM, SSD_STATE))
        st_s.append(st_new_s.reshape(bs, n_heads, SSD_HEADDIM, SSD_STATE))

        cqn, lat, latb, kr, krb = mla_pre(proj, c_small, small_w, mla_q_norm_w[l],
                                          mla_kv_norm_w[l], cos_k, sin_k, tbl_index)
        q = matmul(cqn, wq, F32, bn=1024)
        ql = q_latent(q, w_uk_t)
        qr = q_rope(q, MLA_HEADS * MLA_NOPE, cos_q, sin_q, tbl_index)
        o_lat_p = attn_prompt(ql, qr, latb, krb, bp, sp, scale)
        o_lat_s = attn_sample(page_table, jnp.swapaxes(ql[:, mp:], 0, 1),
                              jnp.swapaxes(qr[:, mp:], 0, 1), lat[mp:, None, :], kr[mp:, None, :],
                              cache_mla_latent, cache_mla_krope, l, scale)
        o_lat = jnp.concatenate([o_lat_p, jnp.swapaxes(o_lat_s, 0, 1)], axis=1)
        o_mla = v_up(o_lat, w_uv_h)
        lat_l.append(lat)
        kr_l.append(kr)

        merged = branch_merge(u_sc, u_ssd, o_mla, sc_w_out[l].astype(BF16),
                              ssd_w_out[l].astype(BF16), mla_w_o[l].astype(BF16),
                              proj, c_gate, gate_b[l])
        x = matmul_residual(merged, w_out[l].astype(BF16), x)
        h2 = rmsnorm_rows(x, norm_mlp_w[l], BF16)
        a = matmul(h2, mlp_w1[l].astype(BF16), BF16, act="relu2", bn=1024)
        x = matmul_residual(a, mlp_w2[l].astype(BF16), x)

    y_prompt = rmsnorm_rows(x, final_norm_w, F32, row0=0, rows=mp).reshape(bp, sp, d)
    y_sample = rmsnorm_rows(x, final_norm_w, F32, row0=mp, rows=bs).reshape(bs, ss, d)
    lat_all = jnp.stack(lat_l, axis=1)
    kr_all = jnp.stack(kr_l, axis=1)
    return (y_prompt, y_sample,
            lat_all[:mp].reshape(bp, sp, depth, kv_lora), kr_all[:mp].reshape(bp, sp, depth, MLA_ROPE),
            lat_all[mp:].reshape(bs, ss, depth, kv_lora), kr_all[mp:].reshape(bs, ss, depth, MLA_ROPE),
            jnp.stack(st_p, axis=0), jnp.stack(st_s, axis=0),
            jnp.stack(cv_p, axis=0), jnp.stack(cv_s, axis=0),
            jnp.stack(sc_p, axis=0), jnp.stack(sc_s, axis=0))
```

```python
import functools
import math

import jax
import jax.numpy as jnp
from jax import lax
from jax.experimental import pallas as pl
from jax.experimental.pallas import tpu as pltpu

F32 = jnp.float32
BF16 = jnp.bfloat16

SC_WIDTH = 3
SSD_HEADDIM = 64
SSD_GROUPS = 8
SSD_STATE = 128
SSD_CONV = 4
SSD_CHUNK = 128
MLA_HEADS = 16
MLA_NOPE = 128
MLA_ROPE = 64
MLA_V = 128
ROPE_THETA = 10000.0
ATT_BLOCK = 256
EPS = 1e-6

LANE = 128
VMEM_LIMIT = 60000 * 1024
LOG2E = 1.4426950408889634


def _cparams(sem, vmem=None):
    return pltpu.CompilerParams(dimension_semantics=sem, vmem_limit_bytes=vmem)


def _row_block(m, target, mult=16):
    best = None
    for d in range(mult, min(m, target) + 1, mult):
        if m % d == 0:
            best = d
    assert best is not None, (m, target)
    return best


def _once(block_shape, index_map):
    return pl.BlockSpec(block_shape, index_map, pipeline_mode=pl.Buffered(1))


def _sigmoid(x):
    return 1.0 / (1.0 + jnp.exp(-x))


def _silu(x):
    return x * _sigmoid(x)


def _softplus(x):
    return jnp.maximum(x, 0.0) + jnp.log1p(jnp.exp(-jnp.abs(x)))


def _dot(a, b):
    return jnp.dot(a, b, preferred_element_type=F32)


def _dot_nt(a, b):
    return lax.dot_general(a, b, (((1,), (1,)), ((), ())), preferred_element_type=F32)


def _rmsnorm_body(x_ref, w_ref, o_ref):
    x = x_ref[...].astype(F32)
    ms = jnp.mean(x * x, axis=-1, keepdims=True)
    o_ref[...] = (x * lax.rsqrt(ms + EPS) * w_ref[...]).astype(o_ref.dtype)


def rmsnorm_rows(x, w, out_dtype, *, row0=0, rows=None, br_target=512):
    m, c = x.shape
    rows = m if rows is None else rows
    br = _row_block(math.gcd(rows, row0) if row0 else rows, br_target)
    off = row0 // br
    return pl.pallas_call(
        _rmsnorm_body,
        grid=(rows // br,),
        in_specs=[pl.BlockSpec((br, c), lambda i: (i + off, 0)),
                  pl.BlockSpec((1, c), lambda i: (0, 0))],
        out_specs=pl.BlockSpec((br, c), lambda i: (i, 0)),
        out_shape=jax.ShapeDtypeStruct((rows, c), out_dtype),
        compiler_params=_cparams(("parallel",), VMEM_LIMIT),
        name="rmsnorm_rows",
    )(x, w.reshape(1, c))


def _mm_body(x_ref, w_ref, o_ref, *, act, w_is_nk):
    if w_is_nk:
        acc = _dot_nt(x_ref[...], w_ref[0].astype(BF16))
    else:
        acc = _dot(x_ref[...], w_ref[...].astype(BF16))
    if act == "relu2":
        r = jnp.maximum(acc, 0.0)
        acc = r * r
    o_ref[...] = acc.astype(o_ref.dtype)


def matmul_w(x, w, layer, out_dtype, *, col0=0, ncols=None, act=None, w_is_nk=False,
             bm_target=2080, bn=512):
    m, k = x.shape
    n_total = w.shape[1] if w_is_nk else w.shape[2]
    ncols = n_total - col0 if ncols is None else ncols
    bm = _row_block(m, bm_target)
    assert ncols % bn == 0
    if w_is_nk:
        assert col0 % 8 == 0
        w_spec = pl.BlockSpec((pl.Element(1), pl.Element(bn), pl.Element(k)),
                              lambda i, j: (layer, pl.multiple_of(col0 + j * bn, 8), 0))
    else:
        assert col0 % bn == 0
        w_spec = pl.BlockSpec((None, k, bn), lambda i, j: (layer, 0, col0 // bn + j))
    return pl.pallas_call(
        functools.partial(_mm_body, act=act, w_is_nk=w_is_nk),
        grid=(m // bm, ncols // bn),
        in_specs=[_once((bm, k), lambda i, j: (i, 0)), w_spec],
        out_specs=pl.BlockSpec((bm, bn), lambda i, j: (i, j)),
        out_shape=jax.ShapeDtypeStruct((m, ncols), out_dtype),
        compiler_params=_cparams(("parallel", "arbitrary"), VMEM_LIMIT),
        name="matmul_w",
    )(x, w)


def _mm_res_body(x_ref, w_ref, r_ref, o_ref):
    kk = pl.program_id(2)
    part = _dot(x_ref[...], w_ref[...].astype(BF16))

    @pl.when(kk == 0)
    def _():
        o_ref[...] = r_ref[...] + part

    @pl.when(kk > 0)
    def _():
        o_ref[...] += part


def matmul_residual(x, w, layer, res, *, bm_target=2080, bn=1024, bk=1024):
    m, k = x.shape
    n = w.shape[2]
    bm = _row_block(m, bm_target)
    assert n % bn == 0 and k % bk == 0
    return pl.pallas_call(
        _mm_res_body,
        grid=(m // bm, n // bn, k // bk),
        in_specs=[pl.BlockSpec((bm, bk), lambda i, j, kk: (i, kk)),
                  pl.BlockSpec((None, bk, bn), lambda i, j, kk: (layer, kk, j)),
                  _once((bm, bn), lambda i, j, kk: (i, j))],
        out_specs=pl.BlockSpec((bm, bn), lambda i, j, kk: (i, j)),
        out_shape=jax.ShapeDtypeStruct((m, n), F32),
        compiler_params=_cparams(("parallel", "parallel", "arbitrary"), VMEM_LIMIT),
        name="matmul_residual",
    )(x, w, res)


def _merge_body(usc_ref, ussd_ref, o_ref, wsc_ref, wssd_ref, wo_ref,
                g0_ref, g1_ref, g2_ref, gb_ref, out_ref):
    def branch(u_ref, w_ref, g_ref, k):
        y = _dot(u_ref[...], w_ref[...].astype(BF16))
        return _sigmoid(g_ref[...] + gb_ref[k:k + 1, :]) * y

    merged = branch(usc_ref, wsc_ref, g0_ref, 0) + branch(ussd_ref, wssd_ref, g1_ref, 1)
    merged = merged + branch(o_ref, wo_ref, g2_ref, 2)
    out_ref[...] = merged.astype(out_ref.dtype)


def branch_merge(u_sc, u_ssd, o_mla, w_sc, w_ssd, w_o, layer, proj, gate_col0, gate_b,
                 *, bm_target=1040, bn=256):
    m, kdim = u_sc.shape
    d = w_sc.shape[2]
    bm = _row_block(m, bm_target)
    goff = gate_col0 // bn
    nd = d // bn
    u_spec = _once((bm, kdim), lambda i, j: (i, 0))
    w_spec = pl.BlockSpec((None, kdim, bn), lambda i, j: (layer, 0, j))

    def g_spec(k):
        return pl.BlockSpec((bm, bn), lambda i, j: (i, goff + k * nd + j))

    return pl.pallas_call(
        _merge_body,
        grid=(m // bm, nd),
        in_specs=[u_spec, u_spec, u_spec, w_spec, w_spec, w_spec,
                  g_spec(0), g_spec(1), g_spec(2),
                  pl.BlockSpec((None, 3, bn), lambda i, j: (layer, 0, j))],
        out_specs=pl.BlockSpec((bm, bn), lambda i, j: (i, j)),
        out_shape=jax.ShapeDtypeStruct((m, d), BF16),
        compiler_params=_cparams(("parallel", "arbitrary"), VMEM_LIMIT),
        name="branch_merge",
    )(u_sc, u_ssd, o_mla, w_sc, w_ssd, w_o, proj, proj, proj, gate_b)


def _head_mm_body(x_ref, w_ref, o_ref):
    o_ref[...] = _dot(x_ref[...].astype(BF16), w_ref[...]).astype(o_ref.dtype)


def q_latent(q, w_uk_t, layer, *, bm_target=1040):
    m = q.shape[0]
    _, h, nope, r = w_uk_t.shape
    bm = _row_block(m, bm_target)
    return pl.pallas_call(
        _head_mm_body,
        grid=(m // bm, h),
        in_specs=[pl.BlockSpec((bm, nope), lambda i, hh: (i, hh)),
                  pl.BlockSpec((None, None, nope, r), lambda i, hh: (layer, hh, 0, 0))],
        out_specs=pl.BlockSpec((None, bm, r), lambda i, hh: (hh, i, 0)),
        out_shape=jax.ShapeDtypeStruct((h, m, r), BF16),
        compiler_params=_cparams(("parallel", "arbitrary")),
        name="q_latent",
    )(q, w_uk_t)


def v_up(o_lat, w_uv_h, layer, *, bm_target=1040):
    h, m, r = o_lat.shape
    v = w_uv_h.shape[3]
    bm = _row_block(m, bm_target)
    return pl.pallas_call(
        _head_mm_body,
        grid=(m // bm, h),
        in_specs=[pl.BlockSpec((None, bm, r), lambda i, hh: (hh, i, 0)),
                  pl.BlockSpec((None, None, r, v), lambda i, hh: (layer, hh, 0, 0))],
        out_specs=pl.BlockSpec((bm, v), lambda i, hh: (i, hh)),
        out_shape=jax.ShapeDtypeStruct((m, h * v), BF16),
        compiler_params=_cparams(("parallel", "arbitrary")),
        name="v_up",
    )(o_lat, w_uv_h)


def _shift_rows(u, k, row):
    return jnp.where(row >= k, pltpu.roll(u, k, 0), 0.0)


def _sc_conv_prompt_body(b_ref, c_ref, x_ref, w_ref, ubuf_ref, u_ref, tail_ref):
    del ubuf_ref
    u = c_ref[...] * x_ref[...]
    seq = u.shape[0]
    row = lax.broadcasted_iota(jnp.int32, u.shape, 0)
    v = _shift_rows(u, 2, row) * w_ref[0:1, :]
    v = v + _shift_rows(u, 1, row) * w_ref[1:2, :]
    v = v + u * w_ref[2:3, :]
    u_ref[...] = (b_ref[...] * v).astype(u_ref.dtype)
    tail_ref[...] = u[seq - (SC_WIDTH - 1):, :]


def sc_conv_prompt(proj, w, batch, seq, u_buf, *, bc=512):
    sc_dim = w.shape[1]
    nc = sc_dim // bc
    return pl.pallas_call(
        _sc_conv_prompt_body,
        grid=(batch, nc),
        in_specs=[pl.BlockSpec((seq, bc), lambda b, j: (b, j)),
                  pl.BlockSpec((seq, bc), lambda b, j: (b, nc + j)),
                  pl.BlockSpec((seq, bc), lambda b, j: (b, 2 * nc + j)),
                  pl.BlockSpec((SC_WIDTH, bc), lambda b, j: (0, j)),
                  pl.BlockSpec(memory_space=pl.ANY)],
        out_specs=[pl.BlockSpec((seq, bc), lambda b, j: (b, j)),
                   pl.BlockSpec((None, SC_WIDTH - 1, bc), lambda b, j: (b, 0, j))],
        out_shape=[jax.ShapeDtypeStruct(u_buf.shape, u_buf.dtype),
                   jax.ShapeDtypeStruct((batch, SC_WIDTH - 1, sc_dim), F32)],
        input_output_aliases={4: 0},
        compiler_params=_cparams(("parallel", "parallel"), VMEM_LIMIT),
        name="sc_conv_prompt",
    )(proj, proj, proj, w, u_buf)


def _sc_conv_sample_body(b_ref, c_ref, x_ref, buf_ref, w_ref, uprev_ref, u_ref, nbuf_ref):
    del uprev_ref
    u = c_ref[...] * x_ref[...]
    v = buf_ref[0] * w_ref[0:1, :]
    v = v + buf_ref[1] * w_ref[1:2, :]
    v = v + u * w_ref[2:3, :]
    u_ref[...] = (b_ref[...] * v).astype(u_ref.dtype)
    nbuf_ref[0] = buf_ref[1]
    nbuf_ref[1] = u


def sc_conv_sample(proj, buf_t, w, u_all, row_block, nb):
    sc_dim = w.shape[1]
    return pl.pallas_call(
        _sc_conv_sample_body,
        grid=(1,),
        in_specs=[pl.BlockSpec((nb, sc_dim), lambda i: (row_block, 0)),
                  pl.BlockSpec((nb, sc_dim), lambda i: (row_block, 1)),
                  pl.BlockSpec((nb, sc_dim), lambda i: (row_block, 2)),
                  pl.BlockSpec((SC_WIDTH - 1, nb, sc_dim), lambda i: (0, 0, 0)),
                  pl.BlockSpec((SC_WIDTH, sc_dim), lambda i: (0, 0)),
                  pl.BlockSpec(memory_space=pl.ANY)],
        out_specs=[pl.BlockSpec((nb, sc_dim), lambda i: (row_block, 0)),
                   pl.BlockSpec((SC_WIDTH - 1, nb, sc_dim), lambda i: (0, 0, 0))],
        out_shape=[jax.ShapeDtypeStruct(u_all.shape, u_all.dtype),
                   jax.ShapeDtypeStruct((SC_WIDTH - 1, nb, sc_dim), F32)],
        input_output_aliases={5: 0},
        compiler_params=_cparams(("arbitrary",)),
        name="sc_conv_sample",
    )(proj, proj, proj, buf_t, w, u_all)


def _ssd_conv_prompt_body(x_ref, w_ref, bias_ref, o_ref, tail_ref):
    u = x_ref[...]
    seq = u.shape[0]
    row = lax.broadcasted_iota(jnp.int32, u.shape, 0)
    v = _shift_rows(u, 3, row) * w_ref[0:1, :]
    v = v + _shift_rows(u, 2, row) * w_ref[1:2, :]
    v = v + _shift_rows(u, 1, row) * w_ref[2:3, :]
    v = v + u * w_ref[3:4, :]
    o_ref[...] = _silu(v + bias_ref[...])
    tail_ref[...] = u[seq - (SSD_CONV - 1):, :]


def ssd_conv_prompt(proj, col0, w, bias, batch, seq, *, bc=512):
    cdim = w.shape[1]
    nc = cdim // bc
    off = col0 // bc
    return pl.pallas_call(
        _ssd_conv_prompt_body,
        grid=(batch, nc),
        in_specs=[pl.BlockSpec((seq, bc), lambda b, j: (b, off + j)),
                  pl.BlockSpec((SSD_CONV, bc), lambda b, j: (0, j)),
                  pl.BlockSpec((1, bc), lambda b, j: (0, j))],
        out_specs=[pl.BlockSpec((seq, bc), lambda b, j: (b, j)),
                   pl.BlockSpec((None, SSD_CONV - 1, bc), lambda b, j: (b, 0, j))],
        out_shape=[jax.ShapeDtypeStruct((batch * seq, cdim), F32),
                   jax.ShapeDtypeStruct((batch, SSD_CONV - 1, cdim), F32)],
        compiler_params=_cparams(("parallel", "parallel"), VMEM_LIMIT),
        name="ssd_conv_prompt",
    )(proj, w, bias.reshape(1, cdim))


def _cumsum_rows(a):
    n = a.shape[0]
    row = lax.broadcasted_iota(jnp.int32, a.shape, 0)
    k = 1
    while k < n:
        a = a + _shift_rows(a, k, row)
        k *= 2
    return a


def _ssd_prompt_body(xs_ref, b_ref, c_ref, z_ref, dt_ref, dtb_ref, alog_ref, d_ref, nw_ref,
                     ubuf_ref, u_ref, st_ref, h_ref, y_ref, *, n_chunks):
    del ubuf_ref
    g = pl.program_id(1)
    c = pl.program_id(2)
    e_per_g = y_ref.shape[1] // SSD_HEADDIM
    q = xs_ref.shape[0]

    @pl.when(c == 0)
    def _():
        h_ref[...] = jnp.zeros_like(h_ref)

    x = xs_ref[...]
    bmat = b_ref[...].astype(BF16)
    cmat = c_ref[...].astype(BF16)
    dt_all = _softplus(dt_ref[...] + dtb_ref[...])
    acum_all = _cumsum_rows(dt_all * (-jnp.exp(alog_ref[...])))
    shift = (LANE - e_per_g * g) % LANE
    dt_g = pltpu.roll(dt_all, shift, 1)
    ac_g = pltpu.roll(acum_all, shift, 1)
    dt_t = dt_g.T
    ac_t = ac_g.T
    x_t = x.T
    cb = _dot_nt(cmat, bmat)
    tri = (lax.broadcasted_iota(jnp.int32, (q, q), 0)
           >= lax.broadcasted_iota(jnp.int32, (q, q), 1))
    for e in range(e_per_g):
        lo, hi = e * SSD_HEADDIM, (e + 1) * SSD_HEADDIM
        a_col = ac_g[:, e:e + 1]
        a_row = ac_t[e:e + 1, :]
        dt_row = dt_t[e:e + 1, :]
        a_last = ac_g[q - 1:q, e:e + 1]
        decay = jnp.exp(jnp.where(tri, a_col - a_row, -jnp.inf))
        w_ts = cb * decay * dt_row
        xe = x[:, lo:hi]
        he = h_ref[lo:hi, :]
        y = _dot(w_ts.astype(BF16), xe.astype(BF16))
        y = y + _dot_nt(cmat, he.astype(BF16)) * jnp.exp(a_col)
        y_ref[:, lo:hi] = y + xe * d_ref[:, lo:hi]
        to_end = jnp.exp(a_last - a_row) * dt_row
        s_chunk = _dot((x_t[lo:hi, :] * to_end).astype(BF16), bmat)
        h_ref[lo:hi, :] = he * jnp.exp(a_last) + s_chunk

    yz = y_ref[...] * _silu(z_ref[...])
    ms = jnp.mean(yz * yz, axis=-1, keepdims=True)
    u_ref[...] = (yz * lax.rsqrt(ms + EPS) * nw_ref[...]).astype(u_ref.dtype)

    @pl.when(c == n_chunks - 1)
    def _():
        st_ref[...] = h_ref[...]


def ssd_prompt(xbc, proj, z_col0, proj_dt, dt_col0, dt_bias, a_log, d_skip, norm_w, batch, seq,
               u_buf):
    ssd_dim = norm_w.shape[0]
    gw = ssd_dim // SSD_GROUPS
    q = SSD_CHUNK
    n_chunks = seq // q
    nb0 = ssd_dim // SSD_STATE
    pad = lambda v: jnp.zeros((1, LANE), F32).at[0, :v.shape[0]].set(v)
    row = lambda b, c: b * n_chunks + c
    return pl.pallas_call(
        functools.partial(_ssd_prompt_body, n_chunks=n_chunks),
        grid=(batch, SSD_GROUPS, n_chunks),
        in_specs=[pl.BlockSpec((q, gw), lambda b, g, c: (row(b, c), g)),
                  pl.BlockSpec((q, SSD_STATE), lambda b, g, c: (row(b, c), nb0 + g)),
                  pl.BlockSpec((q, SSD_STATE), lambda b, g, c: (row(b, c), nb0 + SSD_GROUPS + g)),
                  pl.BlockSpec((q, gw), lambda b, g, c: (row(b, c), z_col0 // gw + g)),
                  pl.BlockSpec((q, LANE), lambda b, g, c: (row(b, c), dt_col0 // LANE)),
                  pl.BlockSpec((1, LANE), lambda b, g, c: (0, 0)),
                  pl.BlockSpec((1, LANE), lambda b, g, c: (0, 0)),
                  pl.BlockSpec((1, gw), lambda b, g, c: (0, g)),
                  pl.BlockSpec((1, gw), lambda b, g, c: (0, g)),
                  pl.BlockSpec(memory_space=pl.ANY)],
        out_specs=[pl.BlockSpec((q, gw), lambda b, g, c: (row(b, c), g)),
                   pl.BlockSpec((None, None, gw, SSD_STATE), lambda b, g, c: (b, g, 0, 0))],
        out_shape=[jax.ShapeDtypeStruct(u_buf.shape, u_buf.dtype),
                   jax.ShapeDtypeStruct((batch, SSD_GROUPS, gw, SSD_STATE), F32)],
        scratch_shapes=[pltpu.VMEM((gw, SSD_STATE), F32), pltpu.VMEM((q, gw), F32)],
        input_output_aliases={9: 0},
        compiler_params=_cparams(("parallel", "parallel", "arbitrary")),
        name="ssd_prompt",
    )(xbc, xbc, xbc, proj, proj_dt, pad(dt_bias), pad(a_log),
      jnp.repeat(d_skip, SSD_HEADDIM).reshape(1, ssd_dim), norm_w.reshape(1, ssd_dim), u_buf)


def _ssd_sample_pre_body(x_ref, buf_ref, w_ref, bias_ref, dt_ref, dtb_ref, alog_ref,
                         act_ref, nbuf_ref, da_t_ref, dtx_t_ref, *, ssd_dim, n_heads):
    u = x_ref[...]
    v = buf_ref[0] * w_ref[0:1, :]
    v = v + buf_ref[1] * w_ref[1:2, :]
    v = v + buf_ref[2] * w_ref[2:3, :]
    v = v + u * w_ref[3:4, :]
    act = _silu(v + bias_ref[...])
    act_ref[...] = act
    nbuf_ref[0] = buf_ref[1]
    nbuf_ref[1] = buf_ref[2]
    nbuf_ref[2] = u
    dt = _softplus(dt_ref[...] + dtb_ref[...])
    da = jnp.exp(dt * (-jnp.exp(alog_ref[...])))
    dt_t = dt.T
    da_t = da.T
    x_t = act[:, :ssd_dim].T
    nb = u.shape[0]
    for h in range(n_heads):
        lo, hi = h * SSD_HEADDIM, (h + 1) * SSD_HEADDIM
        da_t_ref[lo:hi, :] = jnp.broadcast_to(da_t[h:h + 1, :], (SSD_HEADDIM, nb))
        dtx_t_ref[lo:hi, :] = dt_t[h:h + 1, :] * x_t[lo:hi, :]


def ssd_sample_pre(proj, xbc_col0, proj_dt, dt_col0, buf_t, w, bias, dt_bias, a_log,
                   row_block, nb, ssd_dim):
    cdim = w.shape[1]
    n_heads = ssd_dim // SSD_HEADDIM
    pad = lambda v: jnp.zeros((1, LANE), F32).at[0, :v.shape[0]].set(v)
    full = lambda *s: pl.BlockSpec(s, lambda i: (0,) * len(s))
    return pl.pallas_call(
        functools.partial(_ssd_sample_pre_body, ssd_dim=ssd_dim, n_heads=n_heads),
        grid=(1,),
        in_specs=[pl.BlockSpec((nb, cdim), lambda i: (row_block, xbc_col0 // cdim)),
                  full(SSD_CONV - 1, nb, cdim), full(SSD_CONV, cdim), full(1, cdim),
                  pl.BlockSpec((nb, LANE), lambda i: (row_block, dt_col0 // LANE)),
                  full(1, LANE), full(1, LANE)],
        out_specs=[full(nb, cdim), full(SSD_CONV - 1, nb, cdim), full(ssd_dim, nb), full(ssd_dim, nb)],
        out_shape=[jax.ShapeDtypeStruct((nb, cdim), F32),
                   jax.ShapeDtypeStruct((SSD_CONV - 1, nb, cdim), F32),
                   jax.ShapeDtypeStruct((ssd_dim, nb), F32),
                   jax.ShapeDtypeStruct((ssd_dim, nb), F32)],
        compiler_params=_cparams(("arbitrary",), VMEM_LIMIT),
        name="ssd_sample_pre",
    )(proj, buf_t, w, bias.reshape(1, cdim), proj_dt, pad(dt_bias), pad(a_log))


def _ssd_sample_step_body(st_ref, da_t_ref, dtx_t_ref, b_ref, c_ref, nbuf_ref, nst_ref, y_t_ref):
    del nbuf_ref
    b = pl.program_id(0)

    @pl.when(b == 0)
    def _():
        y_t_ref[...] = jnp.zeros_like(y_t_ref)

    rows, nb = da_t_ref.shape
    groups, n = b_ref.shape
    lane = lax.broadcasted_iota(jnp.int32, (rows, nb), 1)
    pick = lambda t: jnp.sum(jnp.where(lane == b, t, 0.0), axis=1, keepdims=True)
    da = pick(da_t_ref[...])
    dtx = pick(dtx_t_ref[...])
    rep = lambda m: jnp.broadcast_to(m[:, None, :], (groups, rows // groups, n)).reshape(rows, n)
    h_new = st_ref[...] * da + dtx * rep(b_ref[...])
    nst_ref[...] = h_new
    y = jnp.sum(h_new * rep(c_ref[...]), axis=1, keepdims=True)
    y_t_ref[...] = jnp.where(lane == b, y, y_t_ref[...])


def ssd_sample_step(state_all, layer, da_t, dtx_t, bmat, cmat, new_all):
    depth, nb, rows, n = state_all.shape
    groups = bmat.shape[1]
    return pl.pallas_call(
        _ssd_sample_step_body,
        grid=(nb,),
        in_specs=[pl.BlockSpec((None, None, rows, n), lambda b: (layer, b, 0, 0)),
                  pl.BlockSpec((rows, nb), lambda b: (0, 0)),
                  pl.BlockSpec((rows, nb), lambda b: (0, 0)),
                  pl.BlockSpec((None, groups, n), lambda b: (b, 0, 0)),
                  pl.BlockSpec((None, groups, n), lambda b: (b, 0, 0)),
                  pl.BlockSpec(memory_space=pl.ANY)],
        out_specs=[pl.BlockSpec((None, None, rows, n), lambda b: (layer, b, 0, 0)),
                   pl.BlockSpec((rows, nb), lambda b: (0, 0))],
        out_shape=[jax.ShapeDtypeStruct((depth, nb, rows, n), F32),
                   jax.ShapeDtypeStruct((rows, nb), F32)],
        input_output_aliases={5: 0},
        compiler_params=_cparams(("arbitrary",)),
        name="ssd_sample_step",
    )(state_all, da_t, dtx_t, bmat, cmat, new_all)


def _ssd_sample_post_body(y_t_ref, x_ref, z_ref, d_ref, nw_ref, uprev_ref, u_ref, *, gw):
    del uprev_ref
    y = y_t_ref[...].T + x_ref[...] * d_ref[...]
    yz = y * _silu(z_ref[...])
    for g in range(yz.shape[1] // gw):
        blk = yz[:, g * gw:(g + 1) * gw]
        ms = jnp.mean(blk * blk, axis=-1, keepdims=True)
        u_ref[:, g * gw:(g + 1) * gw] = (
            blk * lax.rsqrt(ms + EPS) * nw_ref[:, g * gw:(g + 1) * gw]).astype(u_ref.dtype)


def ssd_sample_post(y_t, act, proj, z_col0, d_rep, norm_w, u_all, row_block, nb):
    ssd_dim = norm_w.shape[0]
    gw = ssd_dim // SSD_GROUPS
    full = lambda *s: pl.BlockSpec(s, lambda i: (0,) * len(s))
    return pl.pallas_call(
        functools.partial(_ssd_sample_post_body, gw=gw),
        grid=(1,),
        in_specs=[full(ssd_dim, nb),
                  pl.BlockSpec((nb, ssd_dim), lambda i: (0, 0)),
                  pl.BlockSpec((nb, ssd_dim), lambda i: (row_block, z_col0 // ssd_dim)),
                  full(1, ssd_dim), full(1, ssd_dim),
                  pl.BlockSpec(memory_space=pl.ANY)],
        out_specs=pl.BlockSpec((nb, ssd_dim), lambda i: (row_block, 0)),
        out_shape=jax.ShapeDtypeStruct(u_all.shape, u_all.dtype),
        input_output_aliases={5: 0},
        compiler_params=_cparams(("arbitrary",)),
        name="ssd_sample_post",
    )(y_t, act, proj, d_rep.reshape(1, ssd_dim), norm_w.reshape(1, ssd_dim), u_all)


def _rope_pairs(x, cos, sin_signed, half, lane):
    width = x.shape[1]
    first = (lane % (2 * half)) < half
    partner = jnp.where(first, pltpu.roll(x, width - half, 1), pltpu.roll(x, half, 1))
    return x * cos + partner * sin_signed


def _mla_pre_body(p_ref, qw_ref, kvw_ref, cos_ref, sin_ref,
                  cqn_ref, lat_ref, latb_ref, kr_ref, krb_ref, *, q_lora, kv_lora):
    cq = p_ref[:, :q_lora]
    ms = jnp.mean(cq * cq, axis=-1, keepdims=True)
    cqn_ref[...] = (cq * lax.rsqrt(ms + EPS) * qw_ref[...]).astype(cqn_ref.dtype)
    ckv = p_ref[:, q_lora:q_lora + kv_lora]
    ms = jnp.mean(ckv * ckv, axis=-1, keepdims=True)
    lat = ckv * lax.rsqrt(ms + EPS) * kvw_ref[...]
    lat_ref[...] = lat
    latb_ref[...] = lat.astype(BF16)
    kr = p_ref[:, q_lora + kv_lora:q_lora + kv_lora + LANE]
    lane = lax.broadcasted_iota(jnp.int32, kr.shape, 1)
    rot = _rope_pairs(kr, cos_ref[...], sin_ref[...], MLA_ROPE // 2, lane)[:, :MLA_ROPE]
    kr_ref[...] = rot
    krb_ref[...] = rot.astype(BF16)


def mla_pre(proj, col0, width, q_norm_w, kv_norm_w, cos_k, sin_k, tbl_index, *, br=128):
    m = proj.shape[0]
    q_lora, kv_lora = q_norm_w.shape[0], kv_norm_w.shape[0]
    row = lambda i: (i, 0)
    return pl.pallas_call(
        functools.partial(_mla_pre_body, q_lora=q_lora, kv_lora=kv_lora),
        grid=(m // br,),
        in_specs=[pl.BlockSpec((br, width), lambda i: (i, col0 // width)),
                  pl.BlockSpec((1, q_lora), lambda i: (0, 0)),
                  pl.BlockSpec((1, kv_lora), lambda i: (0, 0)),
                  pl.BlockSpec((br, LANE), lambda i: (tbl_index(i), 0)),
                  pl.BlockSpec((br, LANE), lambda i: (tbl_index(i), 0))],
        out_specs=[pl.BlockSpec((br, q_lora), row), pl.BlockSpec((br, kv_lora), row),
                   pl.BlockSpec((br, kv_lora), row), pl.BlockSpec((br, MLA_ROPE), row),
                   pl.BlockSpec((br, MLA_ROPE), row)],
        out_shape=[jax.ShapeDtypeStruct((m, q_lora), BF16),
                   jax.ShapeDtypeStruct((m, kv_lora), F32),
                   jax.ShapeDtypeStruct((m, kv_lora), BF16),
                   jax.ShapeDtypeStruct((m, MLA_ROPE), F32),
                   jax.ShapeDtypeStruct((m, MLA_ROPE), BF16)],
        compiler_params=_cparams(("parallel",)),
        name="mla_pre",
    )(proj, q_norm_w.reshape(1, q_lora), kv_norm_w.reshape(1, kv_lora), cos_k, sin_k)


def _q_rope_body(q_ref, cos_ref, sin_ref, o_ref):
    x = q_ref[...]
    lane = lax.broadcasted_iota(jnp.int32, x.shape, 1)
    rot = _rope_pairs(x, cos_ref[...], sin_ref[...], MLA_ROPE // 2, lane)
    for h in range(o_ref.shape[0]):
        o_ref[h] = rot[:, h * MLA_ROPE:(h + 1) * MLA_ROPE].astype(o_ref.dtype)


def q_rope(q, col0, cos_q, sin_q, tbl_index, *, br=128):
    m = q.shape[0]
    width = MLA_HEADS * MLA_ROPE
    return pl.pallas_call(
        _q_rope_body,
        grid=(m // br,),
        in_specs=[pl.BlockSpec((br, width), lambda i: (i, col0 // width)),
                  pl.BlockSpec((br, width), lambda i: (tbl_index(i), 0)),
                  pl.BlockSpec((br, width), lambda i: (tbl_index(i), 0))],
        out_specs=pl.BlockSpec((MLA_HEADS, br, MLA_ROPE), lambda i: (0, i, 0)),
        out_shape=jax.ShapeDtypeStruct((MLA_HEADS, m, MLA_ROPE), BF16),
        compiler_params=_cparams(("parallel",)),
        name="q_rope",
    )(q, cos_q, sin_q)


def _lane_fold(s, op):
    out = s[:, :LANE]
    for c in range(1, s.shape[1] // LANE):
        out = op(out, s[:, c * LANE:(c + 1) * LANE])
    return out


def _attn_prompt_body(ql_ref, qr_ref, lat_ref, kr_ref, obuf_ref, o_ref, m_ref, l_ref, acc_ref,
                      *, c_exp):
    del obuf_ref
    qi = pl.program_id(1)
    h, tq, r = ql_ref.shape
    rows = h * tq
    ql = ql_ref[...].reshape(rows, r)
    qr = qr_ref[...].reshape(rows, qr_ref.shape[2])

    def scores(j, masked):
        k0 = pl.multiple_of(j * tq, tq)
        kl = lat_ref[pl.ds(k0, tq), :]
        kr = kr_ref[pl.ds(k0, tq), :]
        s = _dot_nt(ql, kl) + _dot_nt(qr, kr)
        if masked:
            s3 = s.reshape(h, tq, tq)
            ok = (lax.broadcasted_iota(jnp.int32, s3.shape, 2)
                  <= lax.broadcasted_iota(jnp.int32, s3.shape, 1))
            s = jnp.where(ok, s3, -jnp.inf).reshape(rows, tq)
        return s, kl

    m_ref[...] = jnp.full_like(m_ref, -jnp.inf)

    def max_step(j, masked):
        s, _ = scores(j, masked)
        m_ref[...] = jnp.maximum(m_ref[...], _lane_fold(s, jnp.maximum))

    def max_body(j, carry):
        max_step(j, False)
        return carry

    lax.fori_loop(0, qi, max_body, 0)
    max_step(qi, True)
    m_ref[...] = jnp.broadcast_to(jnp.max(m_ref[...], axis=-1, keepdims=True), m_ref.shape)

    l_ref[...] = jnp.zeros_like(l_ref)
    acc_ref[...] = jnp.zeros_like(acc_ref)

    def acc_step(j, masked):
        s, kl = scores(j, masked)
        mb = jnp.concatenate([m_ref[...]] * (tq // LANE), axis=1)
        p = jnp.exp2((s - mb) * c_exp)
        l_ref[...] += _lane_fold(p, jnp.add)
        acc_ref[...] += _dot(p.astype(BF16), kl)

    def acc_body(j, carry):
        acc_step(j, False)
        return carry

    lax.fori_loop(0, qi, acc_body, 0)
    acc_step(qi, True)
    den = jnp.sum(l_ref[...], axis=-1, keepdims=True)
    o_ref[...] = (acc_ref[...] / den).reshape(h, tq, r).astype(o_ref.dtype)


def attn_prompt(ql, qr, latb, krb, batch, seq, scale, o_buf, *, tq=ATT_BLOCK):
    h, m, r = ql.shape
    rope = qr.shape[2]
    nq = seq // tq
    rows = h * tq
    return pl.pallas_call(
        functools.partial(_attn_prompt_body, c_exp=scale * LOG2E),
        grid=(batch, nq),
        in_specs=[pl.BlockSpec((h, tq, r), lambda b, i: (0, b * nq + i, 0)),
                  pl.BlockSpec((h, tq, rope), lambda b, i: (0, b * nq + i, 0)),
                  pl.BlockSpec((seq, r), lambda b, i: (b, 0)),
                  pl.BlockSpec((seq, rope), lambda b, i: (b, 0)),
                  pl.BlockSpec(memory_space=pl.ANY)],
        out_specs=pl.BlockSpec((h, tq, r), lambda b, i: (0, b * nq + i, 0)),
        out_shape=jax.ShapeDtypeStruct(o_buf.shape, o_buf.dtype),
        scratch_shapes=[pltpu.VMEM((rows, LANE), F32), pltpu.VMEM((rows, LANE), F32),
                        pltpu.VMEM((rows, r), F32)],
        input_output_aliases={4: 0},
        compiler_params=_cparams(("parallel", "parallel"), VMEM_LIMIT),
        name="attn_prompt",
    )(ql, qr, latb, krb, o_buf)


def _attn_sample_body(pt_ref, ql_ref, qr_ref, latn_ref, krn_ref, clat_ref, ckr_ref, oprev_ref,
                      o_ref, lbuf, kbuf, obuf, lsem, ksem, *, layer, scale, n_pages, page):
    del oprev_ref
    b = pl.program_id(0)
    nb = pl.num_programs(0)

    def copies(bb, slot):
        out = []
        for p in range(n_pages):
            pg = pt_ref[bb, p]
            out.append(pltpu.make_async_copy(
                clat_ref.at[pg, :, layer, :], lbuf.at[slot, pl.ds(p * page, page), :], lsem.at[slot]))
            out.append(pltpu.make_async_copy(
                ckr_ref.at[pg, layer], kbuf.at[slot, :, pl.ds(p * page, page)], ksem.at[slot]))
        return out

    @pl.when(b == 0)
    def _():
        for cp in copies(0, 0):
            cp.start()

    slot = b % 2

    @pl.when(b + 1 < nb)
    def _():
        for cp in copies(b + 1, 1 - slot):
            cp.start()

    for cp in copies(b, slot):
        cp.wait()

    ql = ql_ref[...]
    qr = qr_ref[...]
    kl = lbuf[slot].astype(BF16)
    kr_t = kbuf[slot].astype(BF16)
    ln = latn_ref[...]
    kn = krn_ref[...]
    lnb = ln.astype(BF16).astype(F32)
    knb = kn.astype(BF16).astype(F32)
    s_past = (_dot_nt(ql, kl) + _dot(qr, kr_t)) * scale
    s_new = (jnp.sum(ql.astype(F32) * lnb, axis=-1, keepdims=True)
             + jnp.sum(qr.astype(F32) * knb, axis=-1, keepdims=True)) * scale
    m = jnp.maximum(jnp.max(s_past, axis=-1, keepdims=True), s_new)
    p_past = jnp.exp(s_past - m)
    p_new = jnp.exp(s_new - m)
    den = jnp.sum(p_past, axis=-1, keepdims=True) + p_new
    num = _dot(p_past.astype(BF16), kl) + p_new.astype(BF16).astype(F32) * lnb
    obuf[:, pl.ds(b, 1), :] = (num / den)[:, None, :]

    @pl.when(b == nb - 1)
    def _():
        o_ref[...] = obuf[...].astype(o_ref.dtype)


def attn_sample(page_table, ql_s, qr_s, lat_new, kr_new, cache_lat, cache_kr_t, o_all,
                layer, scale, row_block):
    nb, h, r = ql_s.shape
    rope = qr_s.shape[2]
    n_pages = page_table.shape[1]
    page = cache_lat.shape[1]
    t = n_pages * page
    grid_spec = pltpu.PrefetchScalarGridSpec(
        num_scalar_prefetch=1,
        grid=(nb,),
        in_specs=[pl.BlockSpec((None, h, r), lambda b, pt: (b, 0, 0)),
                  pl.BlockSpec((None, h, rope), lambda b, pt: (b, 0, 0)),
                  pl.BlockSpec((None, 1, r), lambda b, pt: (b, 0, 0)),
                  pl.BlockSpec((None, 1, rope), lambda b, pt: (b, 0, 0)),
                  pl.BlockSpec(memory_space=pl.ANY),
                  pl.BlockSpec(memory_space=pl.ANY),
                  pl.BlockSpec(memory_space=pl.ANY)],
        out_specs=pl.BlockSpec((h, nb, r), lambda b, pt: (0, row_block, 0)),
        scratch_shapes=[pltpu.VMEM((2, t, r), F32), pltpu.VMEM((2, rope, t), F32),
                        pltpu.VMEM((h, nb, r), F32),
                        pltpu.SemaphoreType.DMA((2,)), pltpu.SemaphoreType.DMA((2,))],
    )
    return pl.pallas_call(
        functools.partial(_attn_sample_body, layer=layer, scale=scale, n_pages=n_pages, page=page),
        grid_spec=grid_spec,
        out_shape=jax.ShapeDtypeStruct(o_all.shape, o_all.dtype),
        input_output_aliases={7: 0},
        compiler_params=_cparams(("arbitrary",), VMEM_LIMIT),
        name="attn_sample",
    )(page_table, ql_s, qr_s, lat_new, kr_new, cache_lat, cache_kr_t, o_all)


def _rope_tables(positions, reps):
    half = MLA_ROPE // 2
    inv = ROPE_THETA ** (-jnp.arange(half, dtype=F32) / half)
    ang = positions.astype(F32)[:, None] * inv[None, :]
    cos = jnp.cos(ang)
    sin = jnp.sin(ang)
    cos = jnp.concatenate([cos, cos], axis=1)
    sin = jnp.concatenate([-sin, sin], axis=1)
    return jnp.tile(cos, (1, reps)), jnp.tile(sin, (1, reps))


def kernel(x_prompt, x_sample, cache_mla_latent, cache_mla_krope, state_ssd, state_ssd_conv, state_short_conv, page_table, norm_mix_w, w_in, gate_b, sc_conv_w, sc_w_out, ssd_conv_w, ssd_conv_b, ssd_dt_bias, ssd_A_log, ssd_D, ssd_norm_w, ssd_w_out, mla_q_norm_w, mla_w_q_up, mla_kv_norm_w, mla_w_uk, mla_w_uv, mla_w_o, w_out, norm_mlp_w, mlp_w1, mlp_w2, final_norm_w):
    bp, sp, d = x_prompt.shape
    bs, ss, _ = x_sample.shape
    assert ss == 1
    depth = w_in.shape[0]
    mp = bp * sp
    m = mp + bs
    n_pages = page_table.shape[1]
    page = cache_mla_latent.shape[1]
    past_len = n_pages * page

    sc_dim = sc_conv_w.shape[2]
    ssd_dim = ssd_norm_w.shape[1]
    conv_dim = ssd_conv_w.shape[2]
    n_heads = ssd_dt_bias.shape[1]
    q_lora = mla_q_norm_w.shape[1]
    kv_lora = mla_kv_norm_w.shape[1]
    scale = (MLA_NOPE + MLA_ROPE) ** -0.5

    sizes = (sc_dim, sc_dim, sc_dim, ssd_dim, conv_dim, n_heads, q_lora, kv_lora, MLA_ROPE, 3 * d)
    starts = [0]
    for s in sizes:
        starts.append(starts[-1] + s)
    (o_scb, o_scc, o_scx, o_z, o_xbc, o_dt, o_cq, o_ckv, o_kr, o_gate, o_end) = starts
    w_in_t = jnp.swapaxes(w_in, 1, 2)
    n_proj1 = o_dt
    c_z, c_xbc = o_z, o_xbc
    assert n_proj1 % 512 == 0 and (3 * d) % 512 == 0
    small_w = q_lora + kv_lora + 2 * LANE
    c_dt = q_lora + kv_lora + LANE
    assert small_w % 512 == 0
    w_small = jnp.concatenate(
        [w_in[:, :, o_cq:o_gate], jnp.zeros((depth, d, LANE - MLA_ROPE), F32),
         w_in[:, :, o_dt:o_cq], jnp.zeros((depth, d, LANE - n_heads), F32)], axis=2).astype(BF16)
    wq = mla_w_q_up.reshape(depth, q_lora, MLA_HEADS, MLA_NOPE + MLA_ROPE)
    wq = jnp.concatenate([wq[..., :MLA_NOPE].reshape(depth, q_lora, -1),
                          wq[..., MLA_NOPE:].reshape(depth, q_lora, -1)], axis=2).astype(BF16)
    w_uk_t = jnp.transpose(mla_w_uk, (0, 2, 3, 1)).astype(BF16)
    w_uv_h = jnp.transpose(mla_w_uv, (0, 2, 1, 3)).astype(BF16)
    cache_kr_t = jnp.transpose(cache_mla_krope, (0, 2, 3, 1))

    br = 128
    assert sp % br == 0 and bs == br and mp % br == 0
    positions = jnp.concatenate([jnp.arange(sp), jnp.full((br,), past_len)])
    cos_q, sin_q = _rope_tables(positions, MLA_HEADS)
    cos_k, sin_k = _rope_tables(positions, 1)
    zpad = jnp.zeros((positions.shape[0], LANE - MLA_ROPE), F32)
    cos_k = jnp.concatenate([cos_k, zpad], axis=1)
    sin_k = jnp.concatenate([sin_k, zpad], axis=1)
    n_pblk = mp // br
    tbl_index = lambda i: jnp.where(i < n_pblk, i % (sp // br), sp // br)

    sample_blk = mp // bs
    x = jnp.concatenate([x_prompt.reshape(mp, d), x_sample.reshape(bs, d)], axis=0)
    state_all = state_ssd.reshape(depth, bs, ssd_dim, SSD_STATE)
    gn = SSD_GROUPS * SSD_STATE

    lat_l, kr_l, sc_p, sc_s, cv_p, cv_s, st_p = [], [], [], [], [], [], []
    st_new_all = jnp.zeros(state_all.shape, F32)
    u_sc = jnp.zeros((m, sc_dim), BF16)
    u_ssd = jnp.zeros((m, ssd_dim), BF16)
    o_lat = jnp.zeros((MLA_HEADS, m, kv_lora), BF16)
    for l in range(depth):
        h = rmsnorm_rows(x, norm_mix_w[l], BF16)
        proj = matmul_w(h, w_in_t, l, F32, ncols=n_proj1, w_is_nk=True)
        gates = matmul_w(h, w_in_t, l, F32, col0=o_gate, ncols=3 * d, w_is_nk=True)
        proj_s = matmul_w(h, w_small, l, F32)

        u_sc, sc_tail = sc_conv_prompt(proj, sc_conv_w[l], bp, sp, u_sc)
        u_sc, sc_nbuf = sc_conv_sample(proj, jnp.swapaxes(state_short_conv[l], 0, 1),
                                       sc_conv_w[l], u_sc, sample_blk, bs)
        sc_p.append(sc_tail)
        sc_s.append(jnp.swapaxes(sc_nbuf, 0, 1))

        xbc_p, cv_tail = ssd_conv_prompt(proj, c_xbc, ssd_conv_w[l], ssd_conv_b[l], bp, sp)
        u_ssd, st_new_p = ssd_prompt(xbc_p, proj, c_z, proj_s, c_dt, ssd_dt_bias[l], ssd_A_log[l],
                                     ssd_D[l], ssd_norm_w[l], bp, sp, u_ssd)
        act_s, cv_nbuf, da_t, dtx_t = ssd_sample_pre(
            proj, c_xbc, proj_s, c_dt, jnp.swapaxes(state_ssd_conv[l], 0, 1), ssd_conv_w[l],
            ssd_conv_b[l], ssd_dt_bias[l], ssd_A_log[l], sample_blk, bs, ssd_dim)
        st_new_all, y_t = ssd_sample_step(
            state_all, l, da_t, dtx_t,
            act_s[:, ssd_dim:ssd_dim + gn].reshape(bs, SSD_GROUPS, SSD_STATE),
            act_s[:, ssd_dim + gn:].reshape(bs, SSD_GROUPS, SSD_STATE), st_new_all)
        u_ssd = ssd_sample_post(y_t, act_s, proj, c_z, jnp.repeat(ssd_D[l], SSD_HEADDIM),
                                ssd_norm_w[l], u_ssd, sample_blk, bs)
        cv_p.append(cv_tail)
        cv_s.append(jnp.swapaxes(cv_nbuf, 0, 1))
        st_p.append(st_new_p.reshape(bp, n_heads, SSD_HEADDIM, SSD_STATE))

        cqn, lat, latb, kr, krb = mla_pre(proj_s, 0, small_w, mla_q_norm_w[l],
                                          mla_kv_norm_w[l], cos_k, sin_k, tbl_index)
        q = matmul_w(cqn, wq, l, F32, bn=1024)
        ql = q_latent(q, w_uk_t, l)
        qr = q_rope(q, MLA_HEADS * MLA_NOPE, cos_q, sin_q, tbl_index)
        o_lat = attn_prompt(ql, qr, latb, krb, bp, sp, scale, o_lat)
        o_lat = attn_sample(page_table, jnp.swapaxes(ql[:, mp:], 0, 1),
                            jnp.swapaxes(qr[:, mp:], 0, 1), lat[mp:, None, :], kr[mp:, None, :],
                            cache_mla_latent, cache_kr_t, o_lat, l, scale, sample_blk)
        o_mla = v_up(o_lat, w_uv_h, l)
        lat_l.append(lat)
        kr_l.append(kr)

        merged = branch_merge(u_sc, u_ssd, o_mla, sc_w_out, ssd_w_out, mla_w_o, l,
                              gates, 0, gate_b)
        x = matmul_residual(merged, w_out, l, x)
        h2 = rmsnorm_rows(x, norm_mlp_w[l], BF16)
        a = matmul_w(h2, mlp_w1, l, BF16, act="relu2")
        x = matmul_residual(a, mlp_w2, l, x)

    y_prompt = rmsnorm_rows(x, final_norm_w, F32, row0=0, rows=mp).reshape(bp, sp, d)
    y_sample = rmsnorm_rows(x, final_norm_w, F32, row0=mp, rows=bs).reshape(bs, ss, d)
    lat_all = jnp.stack(lat_l, axis=1)
    kr_all = jnp.stack(kr_l, axis=1)
    return (y_prompt, y_sample,
            lat_all[:mp].reshape(bp, sp, depth, kv_lora), kr_all[:mp].reshape(bp, sp, depth, MLA_ROPE),
            lat_all[mp:].reshape(bs, ss, depth, kv_lora), kr_all[mp:].reshape(bs, ss, depth, MLA_ROPE),
            jnp.stack(st_p, axis=0), st_new_all.reshape(depth, bs, n_heads, SSD_HEADDIM, SSD_STATE),
            jnp.stack(cv_p, axis=0), jnp.stack(cv_s, axis=0),
            jnp.stack(sc_p, axis=0), jnp.stack(sc_s, axis=0))
```

```python
import functools
import math

import jax
import jax.numpy as jnp
from jax import lax
from jax.experimental import pallas as pl
from jax.experimental.pallas import tpu as pltpu

F32 = jnp.float32
BF16 = jnp.bfloat16

SC_WIDTH = 3
SSD_HEADDIM = 64
SSD_GROUPS = 8
SSD_STATE = 128
SSD_CONV = 4
SSD_CHUNK = 128
MLA_HEADS = 16
MLA_NOPE = 128
MLA_ROPE = 64
MLA_V = 128
ROPE_THETA = 10000.0
ATT_BLOCK = 256
EPS = 1e-6

LANE = 128
VMEM_LIMIT = 60000 * 1024
LOG2E = 1.4426950408889634


def _cparams(sem, vmem=None):
    return pltpu.CompilerParams(dimension_semantics=sem, vmem_limit_bytes=vmem)


def _row_block(m, target, mult=16):
    best = None
    for d in range(mult, min(m, target) + 1, mult):
        if m % d == 0:
            best = d
    assert best is not None, (m, target)
    return best


def _once(block_shape, index_map):
    return pl.BlockSpec(block_shape, index_map, pipeline_mode=pl.Buffered(1))


def _sigmoid(x):
    return 1.0 / (1.0 + jnp.exp(-x))


def _silu(x):
    return x * _sigmoid(x)


def _softplus(x):
    return jnp.maximum(x, 0.0) + jnp.log1p(jnp.exp(-jnp.abs(x)))


def _dot(a, b):
    return jnp.dot(a, b, preferred_element_type=F32)


def _dot_nt(a, b):
    return lax.dot_general(a, b, (((1,), (1,)), ((), ())), preferred_element_type=F32)


def _rmsnorm_body(x_ref, w_ref, o_ref):
    x = x_ref[...].astype(F32)
    ms = jnp.mean(x * x, axis=-1, keepdims=True)
    o_ref[...] = (x * lax.rsqrt(ms + EPS) * w_ref[...]).astype(o_ref.dtype)


def rmsnorm_rows(x, w, out_dtype, *, row0=0, rows=None, br_target=512):
    m, c = x.shape
    rows = m if rows is None else rows
    br = _row_block(math.gcd(rows, row0) if row0 else rows, br_target)
    off = row0 // br
    return pl.pallas_call(
        _rmsnorm_body,
        grid=(rows // br,),
        in_specs=[pl.BlockSpec((br, c), lambda i: (i + off, 0)),
                  pl.BlockSpec((1, c), lambda i: (0, 0))],
        out_specs=pl.BlockSpec((br, c), lambda i: (i, 0)),
        out_shape=jax.ShapeDtypeStruct((rows, c), out_dtype),
        compiler_params=_cparams(("parallel",), VMEM_LIMIT),
        name="rmsnorm_rows",
    )(x, w.reshape(1, c))


def _mm_body(x_ref, w_ref, *rest, act, w_is_nk):
    o_ref = rest[-1]
    if w_is_nk:
        acc = _dot_nt(x_ref[...], w_ref[0].astype(BF16))
    else:
        acc = _dot(x_ref[...], w_ref[...].astype(BF16))
    if act == "relu2":
        r = jnp.maximum(acc, 0.0)
        acc = r * r
    if len(rest) == 2:
        acc = rest[0][...] + acc
    o_ref[...] = acc.astype(o_ref.dtype)


def matmul_w(x, w, layer, out_dtype, *, col0=0, ncols=None, act=None, w_is_nk=False, res=None,
             bm_target=2080, bn=512):
    m, k = x.shape
    n_total = w.shape[1] if w_is_nk else w.shape[2]
    ncols = n_total - col0 if ncols is None else ncols
    bm = _row_block(m, bm_target)
    assert ncols % bn == 0
    if w_is_nk:
        assert col0 % 8 == 0
        w_spec = pl.BlockSpec((pl.Element(1), pl.Element(bn), pl.Element(k)),
                              lambda i, j: (layer, pl.multiple_of(col0 + j * bn, 8), 0))
    else:
        assert col0 % bn == 0
        w_spec = pl.BlockSpec((None, k, bn), lambda i, j: (layer, 0, col0 // bn + j))
    in_specs = [_once((bm, k), lambda i, j: (i, 0)), w_spec]
    args = [x, w]
    if res is not None:
        in_specs.append(pl.BlockSpec((bm, bn), lambda i, j: (i, j)))
        args.append(res)
    return pl.pallas_call(
        functools.partial(_mm_body, act=act, w_is_nk=w_is_nk),
        grid=(m // bm, ncols // bn),
        in_specs=in_specs,
        out_specs=pl.BlockSpec((bm, bn), lambda i, j: (i, j)),
        out_shape=jax.ShapeDtypeStruct((m, ncols), out_dtype),
        compiler_params=_cparams(("parallel", "arbitrary"), VMEM_LIMIT),
        name="matmul_w",
    )(*args)


def _mm_res_body(x_ref, w_ref, r_ref, o_ref):
    kk = pl.program_id(2)
    part = _dot(x_ref[...], w_ref[...].astype(BF16))

    @pl.when(kk == 0)
    def _():
        o_ref[...] = r_ref[...] + part

    @pl.when(kk > 0)
    def _():
        o_ref[...] += part


def matmul_residual(x, w, layer, res, *, bm_target=1040, bn=1024, bk=2048):
    m, k = x.shape
    n = w.shape[2]
    bm = _row_block(m, bm_target)
    assert n % bn == 0 and k % bk == 0
    return pl.pallas_call(
        _mm_res_body,
        grid=(m // bm, n // bn, k // bk),
        in_specs=[pl.BlockSpec((bm, bk), lambda i, j, kk: (i, kk)),
                  pl.BlockSpec((None, bk, bn), lambda i, j, kk: (layer, kk, j)),
                  _once((bm, bn), lambda i, j, kk: (i, j))],
        out_specs=pl.BlockSpec((bm, bn), lambda i, j, kk: (i, j)),
        out_shape=jax.ShapeDtypeStruct((m, n), F32),
        compiler_params=_cparams(("parallel", "parallel", "arbitrary"), VMEM_LIMIT),
        name="matmul_residual",
    )(x, w, res)


def _merge_body(usc_ref, ussd_ref, o_ref, wsc_ref, wssd_ref, wo_ref,
                g0_ref, g1_ref, g2_ref, gb_ref, out_ref):
    def branch(u_ref, w_ref, g_ref, k):
        y = _dot(u_ref[...], w_ref[...].astype(BF16))
        return _sigmoid(g_ref[...] + gb_ref[k:k + 1, :]) * y

    merged = branch(usc_ref, wsc_ref, g0_ref, 0) + branch(ussd_ref, wssd_ref, g1_ref, 1)
    merged = merged + branch(o_ref, wo_ref, g2_ref, 2)
    out_ref[...] = merged.astype(out_ref.dtype)


def branch_merge(u_sc, u_ssd, o_mla, w_sc, w_ssd, w_o, layer, proj, gate_col0, gate_b,
                 *, bm_target=1040, bn=256):
    m, kdim = u_sc.shape
    d = w_sc.shape[2]
    bm = _row_block(m, bm_target)
    goff = gate_col0 // bn
    nd = d // bn
    u_spec = _once((bm, kdim), lambda i, j: (i, 0))
    w_spec = pl.BlockSpec((None, kdim, bn), lambda i, j: (layer, 0, j))

    def g_spec(k):
        return pl.BlockSpec((bm, bn), lambda i, j: (i, goff + k * nd + j))

    return pl.pallas_call(
        _merge_body,
        grid=(m // bm, nd),
        in_specs=[u_spec, u_spec, u_spec, w_spec, w_spec, w_spec,
                  g_spec(0), g_spec(1), g_spec(2),
                  pl.BlockSpec((None, 3, bn), lambda i, j: (layer, 0, j))],
        out_specs=pl.BlockSpec((bm, bn), lambda i, j: (i, j)),
        out_shape=jax.ShapeDtypeStruct((m, d), BF16),
        compiler_params=_cparams(("parallel", "arbitrary"), VMEM_LIMIT),
        name="branch_merge",
    )(u_sc, u_ssd, o_mla, w_sc, w_ssd, w_o, proj, proj, proj, gate_b)


def _q_latent_body(x_ref, w_ref, o_ref):
    h, nope, _ = w_ref.shape
    for hh in range(h):
        xh = x_ref[:, hh * nope:(hh + 1) * nope].astype(BF16)
        o_ref[hh] = _dot(xh, w_ref[hh]).astype(o_ref.dtype)


def q_latent(q, w_uk_t, layer, *, bm_target=1040):
    m = q.shape[0]
    _, h, nope, r = w_uk_t.shape
    bm = _row_block(m, bm_target)
    return pl.pallas_call(
        _q_latent_body,
        grid=(m // bm,),
        in_specs=[pl.BlockSpec((bm, h * nope), lambda i: (i, 0)),
                  pl.BlockSpec((None, h, nope, r), lambda i: (layer, 0, 0, 0))],
        out_specs=pl.BlockSpec((h, bm, r), lambda i: (0, i, 0)),
        out_shape=jax.ShapeDtypeStruct((h, m, r), BF16),
        compiler_params=_cparams(("parallel",), VMEM_LIMIT),
        name="q_latent",
    )(q, w_uk_t)


def _v_up_body(x_ref, w_ref, o_ref):
    h, _, v = w_ref.shape
    for hh in range(h):
        o_ref[:, hh * v:(hh + 1) * v] = _dot(x_ref[hh], w_ref[hh]).astype(o_ref.dtype)


def v_up(o_lat, w_uv_h, layer, *, bm_target=1040):
    h, m, r = o_lat.shape
    v = w_uv_h.shape[3]
    bm = _row_block(m, bm_target)
    return pl.pallas_call(
        _v_up_body,
        grid=(m // bm,),
        in_specs=[pl.BlockSpec((h, bm, r), lambda i: (0, i, 0)),
                  pl.BlockSpec((None, h, r, v), lambda i: (layer, 0, 0, 0))],
        out_specs=pl.BlockSpec((bm, h * v), lambda i: (i, 0)),
        out_shape=jax.ShapeDtypeStruct((m, h * v), BF16),
        compiler_params=_cparams(("parallel",), VMEM_LIMIT),
        name="v_up",
    )(o_lat, w_uv_h)


def _shift_rows(u, k, row):
    return jnp.where(row >= k, pltpu.roll(u, k, 0), 0.0)


def _sc_conv_prompt_body(b_ref, c_ref, x_ref, w_ref, ubuf_ref, u_ref, tail_ref):
    del ubuf_ref
    u = c_ref[...] * x_ref[...]
    seq = u.shape[0]
    row = lax.broadcasted_iota(jnp.int32, u.shape, 0)
    v = _shift_rows(u, 2, row) * w_ref[0:1, :]
    v = v + _shift_rows(u, 1, row) * w_ref[1:2, :]
    v = v + u * w_ref[2:3, :]
    u_ref[...] = (b_ref[...] * v).astype(u_ref.dtype)
    tail_ref[...] = u[seq - (SC_WIDTH - 1):, :]


def sc_conv_prompt(proj, w, batch, seq, u_buf, *, bc=512):
    sc_dim = w.shape[1]
    nc = sc_dim // bc
    return pl.pallas_call(
        _sc_conv_prompt_body,
        grid=(batch, nc),
        in_specs=[pl.BlockSpec((seq, bc), lambda b, j: (b, j)),
                  pl.BlockSpec((seq, bc), lambda b, j: (b, nc + j)),
                  pl.BlockSpec((seq, bc), lambda b, j: (b, 2 * nc + j)),
                  pl.BlockSpec((SC_WIDTH, bc), lambda b, j: (0, j)),
                  pl.BlockSpec(memory_space=pl.ANY)],
        out_specs=[pl.BlockSpec((seq, bc), lambda b, j: (b, j)),
                   pl.BlockSpec((None, SC_WIDTH - 1, bc), lambda b, j: (b, 0, j))],
        out_shape=[jax.ShapeDtypeStruct(u_buf.shape, u_buf.dtype),
                   jax.ShapeDtypeStruct((batch, SC_WIDTH - 1, sc_dim), F32)],
        input_output_aliases={4: 0},
        compiler_params=_cparams(("parallel", "parallel"), VMEM_LIMIT),
        name="sc_conv_prompt",
    )(proj, proj, proj, w, u_buf)


def _sc_conv_sample_body(b_ref, c_ref, x_ref, buf_ref, w_ref, uprev_ref, u_ref, nbuf_ref):
    del uprev_ref
    u = c_ref[...] * x_ref[...]
    v = buf_ref[0] * w_ref[0:1, :]
    v = v + buf_ref[1] * w_ref[1:2, :]
    v = v + u * w_ref[2:3, :]
    u_ref[...] = (b_ref[...] * v).astype(u_ref.dtype)
    nbuf_ref[0] = buf_ref[1]
    nbuf_ref[1] = u


def sc_conv_sample(proj, buf_t, w, u_all, row_block, nb):
    sc_dim = w.shape[1]
    return pl.pallas_call(
        _sc_conv_sample_body,
        grid=(1,),
        in_specs=[pl.BlockSpec((nb, sc_dim), lambda i: (row_block, 0)),
                  pl.BlockSpec((nb, sc_dim), lambda i: (row_block, 1)),
                  pl.BlockSpec((nb, sc_dim), lambda i: (row_block, 2)),
                  pl.BlockSpec((SC_WIDTH - 1, nb, sc_dim), lambda i: (0, 0, 0)),
                  pl.BlockSpec((SC_WIDTH, sc_dim), lambda i: (0, 0)),
                  pl.BlockSpec(memory_space=pl.ANY)],
        out_specs=[pl.BlockSpec((nb, sc_dim), lambda i: (row_block, 0)),
                   pl.BlockSpec((SC_WIDTH - 1, nb, sc_dim), lambda i: (0, 0, 0))],
        out_shape=[jax.ShapeDtypeStruct(u_all.shape, u_all.dtype),
                   jax.ShapeDtypeStruct((SC_WIDTH - 1, nb, sc_dim), F32)],
        input_output_aliases={5: 0},
        compiler_params=_cparams(("arbitrary",)),
        name="sc_conv_sample",
    )(proj, proj, proj, buf_t, w, u_all)


def _ssd_conv_prompt_body(x_ref, w_ref, bias_ref, o_ref, tail_ref):
    u = x_ref[...]
    seq = u.shape[0]
    row = lax.broadcasted_iota(jnp.int32, u.shape, 0)
    v = _shift_rows(u, 3, row) * w_ref[0:1, :]
    v = v + _shift_rows(u, 2, row) * w_ref[1:2, :]
    v = v + _shift_rows(u, 1, row) * w_ref[2:3, :]
    v = v + u * w_ref[3:4, :]
    o_ref[...] = _silu(v + bias_ref[...])
    tail_ref[...] = u[seq - (SSD_CONV - 1):, :]


def ssd_conv_prompt(proj, col0, w, bias, batch, seq, *, bc=512):
    cdim = w.shape[1]
    nc = cdim // bc
    off = col0 // bc
    return pl.pallas_call(
        _ssd_conv_prompt_body,
        grid=(batch, nc),
        in_specs=[pl.BlockSpec((seq, bc), lambda b, j: (b, off + j)),
                  pl.BlockSpec((SSD_CONV, bc), lambda b, j: (0, j)),
                  pl.BlockSpec((1, bc), lambda b, j: (0, j))],
        out_specs=[pl.BlockSpec((seq, bc), lambda b, j: (b, j)),
                   pl.BlockSpec((None, SSD_CONV - 1, bc), lambda b, j: (b, 0, j))],
        out_shape=[jax.ShapeDtypeStruct((batch * seq, cdim), F32),
                   jax.ShapeDtypeStruct((batch, SSD_CONV - 1, cdim), F32)],
        compiler_params=_cparams(("parallel", "parallel"), VMEM_LIMIT),
        name="ssd_conv_prompt",
    )(proj, w, bias.reshape(1, cdim))


def _cumsum_rows(a):
    n = a.shape[0]
    row = lax.broadcasted_iota(jnp.int32, a.shape, 0)
    k = 1
    while k < n:
        a = a + _shift_rows(a, k, row)
        k *= 2
    return a


def _ssd_prompt_body(xs_ref, b_ref, c_ref, z_ref, dt_ref, dtb_ref, alog_ref, d_ref, nw_ref,
                     ubuf_ref, u_ref, st_ref, h_ref, y_ref, *, n_chunks, gw):
    del ubuf_ref
    gb = pl.program_id(1)
    c = pl.program_id(2)
    gpb = y_ref.shape[1] // gw
    e_per_g = gw // SSD_HEADDIM
    q = xs_ref.shape[0]
    n = SSD_STATE

    @pl.when(c == 0)
    def _():
        h_ref[...] = jnp.zeros_like(h_ref)

    x = xs_ref[...]
    dt_all = _softplus(dt_ref[...] + dtb_ref[...])
    acum_all = _cumsum_rows(dt_all * (-jnp.exp(alog_ref[...])))
    shift = (LANE - gpb * e_per_g * gb) % LANE
    dt_g = pltpu.roll(dt_all, shift, 1)
    ac_g = pltpu.roll(acum_all, shift, 1)
    dt_t = dt_g.T
    ac_t = ac_g.T
    x_t = x.T
    tri = (lax.broadcasted_iota(jnp.int32, (q, q), 0)
           >= lax.broadcasted_iota(jnp.int32, (q, q), 1))
    for gi in range(gpb):
        bmat = b_ref[:, gi * n:(gi + 1) * n].astype(BF16)
        cmat = c_ref[:, gi * n:(gi + 1) * n].astype(BF16)
        cb = _dot_nt(cmat, bmat)
        for e in range(e_per_g):
            hd = gi * e_per_g + e
            lo, hi = hd * SSD_HEADDIM, (hd + 1) * SSD_HEADDIM
            a_col = ac_g[:, hd:hd + 1]
            a_row = ac_t[hd:hd + 1, :]
            dt_row = dt_t[hd:hd + 1, :]
            a_last = ac_g[q - 1:q, hd:hd + 1]
            decay = jnp.exp(jnp.where(tri, a_col - a_row, -jnp.inf))
            w_ts = cb * decay * dt_row
            xe = x[:, lo:hi]
            he = h_ref[lo:hi, :]
            y = _dot(w_ts.astype(BF16), xe.astype(BF16))
            y = y + _dot_nt(cmat, he.astype(BF16)) * jnp.exp(a_col)
            y_ref[:, lo:hi] = y + xe * d_ref[:, lo:hi]
            to_end = jnp.exp(a_last - a_row) * dt_row
            s_chunk = _dot((x_t[lo:hi, :] * to_end).astype(BF16), bmat)
            h_ref[lo:hi, :] = he * jnp.exp(a_last) + s_chunk

    for gi in range(gpb):
        sl = slice(gi * gw, (gi + 1) * gw)
        yz = y_ref[:, sl] * _silu(z_ref[:, sl])
        ms = jnp.mean(yz * yz, axis=-1, keepdims=True)
        u_ref[:, sl] = (yz * lax.rsqrt(ms + EPS) * nw_ref[:, sl]).astype(u_ref.dtype)

    @pl.when(c == n_chunks - 1)
    def _():
        for gi in range(gpb):
            st_ref[gi] = h_ref[gi * gw:(gi + 1) * gw, :]


def ssd_prompt(xbc, proj, z_col0, proj_dt, dt_col0, dt_bias, a_log, d_skip, norm_w, batch, seq,
               u_buf):
    ssd_dim = norm_w.shape[0]
    gw = ssd_dim // SSD_GROUPS
    gpb = 2
    bw = gpb * gw
    bn_ = gpb * SSD_STATE
    q = SSD_CHUNK
    n_chunks = seq // q
    assert SSD_GROUPS % gpb == 0 and ssd_dim % bn_ == 0 and z_col0 % bw == 0
    b_blk0 = ssd_dim // bn_
    c_blk0 = b_blk0 + SSD_GROUPS // gpb
    pad = lambda v: jnp.zeros((1, LANE), F32).at[0, :v.shape[0]].set(v)
    row = lambda b, c: b * n_chunks + c
    return pl.pallas_call(
        functools.partial(_ssd_prompt_body, n_chunks=n_chunks, gw=gw),
        grid=(batch, SSD_GROUPS // gpb, n_chunks),
        in_specs=[pl.BlockSpec((q, bw), lambda b, g, c: (row(b, c), g)),
                  pl.BlockSpec((q, bn_), lambda b, g, c: (row(b, c), b_blk0 + g)),
                  pl.BlockSpec((q, bn_), lambda b, g, c: (row(b, c), c_blk0 + g)),
                  pl.BlockSpec((q, bw), lambda b, g, c: (row(b, c), z_col0 // bw + g)),
                  pl.BlockSpec((q, LANE), lambda b, g, c: (row(b, c), dt_col0 // LANE)),
                  pl.BlockSpec((1, LANE), lambda b, g, c: (0, 0)),
                  pl.BlockSpec((1, LANE), lambda b, g, c: (0, 0)),
                  pl.BlockSpec((1, bw), lambda b, g, c: (0, g)),
                  pl.BlockSpec((1, bw), lambda b, g, c: (0, g)),
                  pl.BlockSpec(memory_space=pl.ANY)],
        out_specs=[pl.BlockSpec((q, bw), lambda b, g, c: (row(b, c), g)),
                   pl.BlockSpec((None, gpb, gw, SSD_STATE), lambda b, g, c: (b, g, 0, 0))],
        out_shape=[jax.ShapeDtypeStruct(u_buf.shape, u_buf.dtype),
                   jax.ShapeDtypeStruct((batch, SSD_GROUPS, gw, SSD_STATE), F32)],
        scratch_shapes=[pltpu.VMEM((bw, SSD_STATE), F32), pltpu.VMEM((q, bw), F32)],
        input_output_aliases={9: 0},
        compiler_params=_cparams(("parallel", "parallel", "arbitrary")),
        name="ssd_prompt",
    )(xbc, xbc, xbc, proj, proj_dt, pad(dt_bias), pad(a_log),
      jnp.repeat(d_skip, SSD_HEADDIM).reshape(1, ssd_dim), norm_w.reshape(1, ssd_dim), u_buf)


def _ssd_sample_pre_body(x_ref, buf_ref, w_ref, bias_ref, dt_ref, dtb_ref, alog_ref,
                         act_ref, nbuf_ref, da_t_ref, dtx_t_ref, *, ssd_dim, n_heads):
    u = x_ref[...]
    v = buf_ref[0] * w_ref[0:1, :]
    v = v + buf_ref[1] * w_ref[1:2, :]
    v = v + buf_ref[2] * w_ref[2:3, :]
    v = v + u * w_ref[3:4, :]
    act = _silu(v + bias_ref[...])
    act_ref[...] = act
    nbuf_ref[0] = buf_ref[1]
    nbuf_ref[1] = buf_ref[2]
    nbuf_ref[2] = u
    dt = _softplus(dt_ref[...] + dtb_ref[...])
    da = jnp.exp(dt * (-jnp.exp(alog_ref[...])))
    dt_t = dt.T
    da_t = da.T
    x_t = act[:, :ssd_dim].T
    nb = u.shape[0]
    for h in range(n_heads):
        lo, hi = h * SSD_HEADDIM, (h + 1) * SSD_HEADDIM
        da_t_ref[lo:hi, :] = jnp.broadcast_to(da_t[h:h + 1, :], (SSD_HEADDIM, nb))
        dtx_t_ref[lo:hi, :] = dt_t[h:h + 1, :] * x_t[lo:hi, :]


def ssd_sample_pre(proj, xbc_col0, proj_dt, dt_col0, buf_t, w, bias, dt_bias, a_log,
                   row_block, nb, ssd_dim):
    cdim = w.shape[1]
    n_heads = ssd_dim // SSD_HEADDIM
    pad = lambda v: jnp.zeros((1, LANE), F32).at[0, :v.shape[0]].set(v)
    full = lambda *s: pl.BlockSpec(s, lambda i: (0,) * len(s))
    return pl.pallas_call(
        functools.partial(_ssd_sample_pre_body, ssd_dim=ssd_dim, n_heads=n_heads),
        grid=(1,),
        in_specs=[pl.BlockSpec((nb, cdim), lambda i: (row_block, xbc_col0 // cdim)),
                  full(SSD_CONV - 1, nb, cdim), full(SSD_CONV, cdim), full(1, cdim),
                  pl.BlockSpec((nb, LANE), lambda i: (row_block, dt_col0 // LANE)),
                  full(1, LANE), full(1, LANE)],
        out_specs=[full(nb, cdim), full(SSD_CONV - 1, nb, cdim), full(ssd_dim, nb), full(ssd_dim, nb)],
        out_shape=[jax.ShapeDtypeStruct((nb, cdim), F32),
                   jax.ShapeDtypeStruct((SSD_CONV - 1, nb, cdim), F32),
                   jax.ShapeDtypeStruct((ssd_dim, nb), F32),
                   jax.ShapeDtypeStruct((ssd_dim, nb), F32)],
        compiler_params=_cparams(("arbitrary",), VMEM_LIMIT),
        name="ssd_sample_pre",
    )(proj, buf_t, w, bias.reshape(1, cdim), proj_dt, pad(dt_bias), pad(a_log))


def _ssd_sample_step_body(st_ref, da_t_ref, dtx_t_ref, b_ref, c_ref, nbuf_ref, nst_ref, y_t_ref):
    del nbuf_ref
    b = pl.program_id(0)

    @pl.when(b == 0)
    def _():
        y_t_ref[...] = jnp.zeros_like(y_t_ref)

    rows, nb = da_t_ref.shape
    groups, n = b_ref.shape
    lane = lax.broadcasted_iota(jnp.int32, (rows, nb), 1)
    pick = lambda t: jnp.sum(jnp.where(lane == b, t, 0.0), axis=1, keepdims=True)
    da = pick(da_t_ref[...])
    dtx = pick(dtx_t_ref[...])
    rep = lambda m: jnp.broadcast_to(m[:, None, :], (groups, rows // groups, n)).reshape(rows, n)
    h_new = st_ref[...] * da + dtx * rep(b_ref[...])
    nst_ref[...] = h_new
    y = jnp.sum(h_new * rep(c_ref[...]), axis=1, keepdims=True)
    y_t_ref[...] = jnp.where(lane == b, y, y_t_ref[...])


def ssd_sample_step(state_all, layer, da_t, dtx_t, bmat, cmat, new_all):
    depth, nb, rows, n = state_all.shape
    groups = bmat.shape[1]
    return pl.pallas_call(
        _ssd_sample_step_body,
        grid=(nb,),
        in_specs=[pl.BlockSpec((None, None, rows, n), lambda b: (layer, b, 0, 0)),
                  pl.BlockSpec((rows, nb), lambda b: (0, 0)),
                  pl.BlockSpec((rows, nb), lambda b: (0, 0)),
                  pl.BlockSpec((None, groups, n), lambda b: (b, 0, 0)),
                  pl.BlockSpec((None, groups, n), lambda b: (b, 0, 0)),
                  pl.BlockSpec(memory_space=pl.ANY)],
        out_specs=[pl.BlockSpec((None, None, rows, n), lambda b: (layer, b, 0, 0)),
                   pl.BlockSpec((rows, nb), lambda b: (0, 0))],
        out_shape=[jax.ShapeDtypeStruct((depth, nb, rows, n), F32),
                   jax.ShapeDtypeStruct((rows, nb), F32)],
        input_output_aliases={5: 0},
        compiler_params=_cparams(("arbitrary",)),
        name="ssd_sample_step",
    )(state_all, da_t, dtx_t, bmat, cmat, new_all)


def _ssd_sample_post_body(y_t_ref, x_ref, z_ref, d_ref, nw_ref, uprev_ref, u_ref, *, gw):
    del uprev_ref
    y = y_t_ref[...].T + x_ref[...] * d_ref[...]
    yz = y * _silu(z_ref[...])
    for g in range(yz.shape[1] // gw):
        blk = yz[:, g * gw:(g + 1) * gw]
        ms = jnp.mean(blk * blk, axis=-1, keepdims=True)
        u_ref[:, g * gw:(g + 1) * gw] = (
            blk * lax.rsqrt(ms + EPS) * nw_ref[:, g * gw:(g + 1) * gw]).astype(u_ref.dtype)


def ssd_sample_post(y_t, act, proj, z_col0, d_rep, norm_w, u_all, row_block, nb):
    ssd_dim = norm_w.shape[0]
    gw = ssd_dim // SSD_GROUPS
    full = lambda *s: pl.BlockSpec(s, lambda i: (0,) * len(s))
    return pl.pallas_call(
        functools.partial(_ssd_sample_post_body, gw=gw),
        grid=(1,),
        in_specs=[full(ssd_dim, nb),
                  pl.BlockSpec((nb, ssd_dim), lambda i: (0, 0)),
                  pl.BlockSpec((nb, ssd_dim), lambda i: (row_block, z_col0 // ssd_dim)),
                  full(1, ssd_dim), full(1, ssd_dim),
                  pl.BlockSpec(memory_space=pl.ANY)],
        out_specs=pl.BlockSpec((nb, ssd_dim), lambda i: (row_block, 0)),
        out_shape=jax.ShapeDtypeStruct(u_all.shape, u_all.dtype),
        input_output_aliases={5: 0},
        compiler_params=_cparams(("arbitrary",)),
        name="ssd_sample_post",
    )(y_t, act, proj, d_rep.reshape(1, ssd_dim), norm_w.reshape(1, ssd_dim), u_all)


def _rope_pairs(x, cos, sin_signed, half, lane):
    width = x.shape[1]
    first = (lane % (2 * half)) < half
    partner = jnp.where(first, pltpu.roll(x, width - half, 1), pltpu.roll(x, half, 1))
    return x * cos + partner * sin_signed


def _mla_pre_body(p_ref, qw_ref, kvw_ref, cos_ref, sin_ref,
                  cqn_ref, lat_ref, latb_ref, kr_ref, krb_ref, *, q_lora, kv_lora):
    cq = p_ref[:, :q_lora]
    ms = jnp.mean(cq * cq, axis=-1, keepdims=True)
    cqn_ref[...] = (cq * lax.rsqrt(ms + EPS) * qw_ref[...]).astype(cqn_ref.dtype)
    ckv = p_ref[:, q_lora:q_lora + kv_lora]
    ms = jnp.mean(ckv * ckv, axis=-1, keepdims=True)
    lat = ckv * lax.rsqrt(ms + EPS) * kvw_ref[...]
    lat_ref[...] = lat
    latb_ref[...] = lat.astype(BF16)
    kr = p_ref[:, q_lora + kv_lora:q_lora + kv_lora + LANE]
    lane = lax.broadcasted_iota(jnp.int32, kr.shape, 1)
    rot = _rope_pairs(kr, cos_ref[...], sin_ref[...], MLA_ROPE // 2, lane)[:, :MLA_ROPE]
    kr_ref[...] = rot
    krb_ref[...] = rot.astype(BF16)


def mla_pre(proj, col0, width, q_norm_w, kv_norm_w, cos_k, sin_k, tbl_index, *, br=128):
    m = proj.shape[0]
    q_lora, kv_lora = q_norm_w.shape[0], kv_norm_w.shape[0]
    row = lambda i: (i, 0)
    return pl.pallas_call(
        functools.partial(_mla_pre_body, q_lora=q_lora, kv_lora=kv_lora),
        grid=(m // br,),
        in_specs=[pl.BlockSpec((br, width), lambda i: (i, col0 // width)),
                  pl.BlockSpec((1, q_lora), lambda i: (0, 0)),
                  pl.BlockSpec((1, kv_lora), lambda i: (0, 0)),
                  pl.BlockSpec((br, LANE), lambda i: (tbl_index(i), 0)),
                  pl.BlockSpec((br, LANE), lambda i: (tbl_index(i), 0))],
        out_specs=[pl.BlockSpec((br, q_lora), row), pl.BlockSpec((br, kv_lora), row),
                   pl.BlockSpec((br, kv_lora), row), pl.BlockSpec((br, MLA_ROPE), row),
                   pl.BlockSpec((br, MLA_ROPE), row)],
        out_shape=[jax.ShapeDtypeStruct((m, q_lora), BF16),
                   jax.ShapeDtypeStruct((m, kv_lora), F32),
                   jax.ShapeDtypeStruct((m, kv_lora), BF16),
                   jax.ShapeDtypeStruct((m, MLA_ROPE), F32),
                   jax.ShapeDtypeStruct((m, MLA_ROPE), BF16)],
        compiler_params=_cparams(("parallel",)),
        name="mla_pre",
    )(proj, q_norm_w.reshape(1, q_lora), kv_norm_w.reshape(1, kv_lora), cos_k, sin_k)


def _q_rope_body(q_ref, cos_ref, sin_ref, o_ref):
    x = q_ref[...]
    lane = lax.broadcasted_iota(jnp.int32, x.shape, 1)
    rot = _rope_pairs(x, cos_ref[...], sin_ref[...], MLA_ROPE // 2, lane)
    for h in range(o_ref.shape[0]):
        o_ref[h] = rot[:, h * MLA_ROPE:(h + 1) * MLA_ROPE].astype(o_ref.dtype)


def q_rope(q, col0, cos_q, sin_q, tbl_index, *, br=128):
    m = q.shape[0]
    width = MLA_HEADS * MLA_ROPE
    return pl.pallas_call(
        _q_rope_body,
        grid=(m // br,),
        in_specs=[pl.BlockSpec((br, width), lambda i: (i, col0 // width)),
                  pl.BlockSpec((br, width), lambda i: (tbl_index(i), 0)),
                  pl.BlockSpec((br, width), lambda i: (tbl_index(i), 0))],
        out_specs=pl.BlockSpec((MLA_HEADS, br, MLA_ROPE), lambda i: (0, i, 0)),
        out_shape=jax.ShapeDtypeStruct((MLA_HEADS, m, MLA_ROPE), BF16),
        compiler_params=_cparams(("parallel",)),
        name="q_rope",
    )(q, cos_q, sin_q)


def _lane_fold(s, op):
    out = s[:, :LANE]
    for c in range(1, s.shape[1] // LANE):
        out = op(out, s[:, c * LANE:(c + 1) * LANE])
    return out


def _attn_prompt_body(ql_ref, qr_ref, lat_ref, kr_ref, obuf_ref, o_ref, m_ref, l_ref, acc_ref,
                      *, c_exp):
    del obuf_ref
    qi = pl.program_id(1)
    h, tq, r = ql_ref.shape
    rows = h * tq
    ql = ql_ref[...].reshape(rows, r)
    qr = qr_ref[...].reshape(rows, qr_ref.shape[2])

    def scores(j, masked):
        k0 = pl.multiple_of(j * tq, tq)
        kl = lat_ref[pl.ds(k0, tq), :]
        kr = kr_ref[pl.ds(k0, tq), :]
        s = _dot_nt(ql, kl) + _dot_nt(qr, kr)
        if masked:
            s3 = s.reshape(h, tq, tq)
            ok = (lax.broadcasted_iota(jnp.int32, s3.shape, 2)
                  <= lax.broadcasted_iota(jnp.int32, s3.shape, 1))
            s = jnp.where(ok, s3, -jnp.inf).reshape(rows, tq)
        return s, kl

    m_ref[...] = jnp.full_like(m_ref, -jnp.inf)

    def max_step(j, masked):
        s, _ = scores(j, masked)
        m_ref[...] = jnp.maximum(m_ref[...], _lane_fold(s, jnp.maximum))

    def max_body(j, carry):
        max_step(j, False)
        return carry

    lax.fori_loop(0, qi, max_body, 0)
    max_step(qi, True)
    m_ref[...] = jnp.broadcast_to(jnp.max(m_ref[...], axis=-1, keepdims=True), m_ref.shape)

    l_ref[...] = jnp.zeros_like(l_ref)
    acc_ref[...] = jnp.zeros_like(acc_ref)

    def acc_step(j, masked):
        s, kl = scores(j, masked)
        mb = jnp.concatenate([m_ref[...]] * (tq // LANE), axis=1)
        p = jnp.exp2((s - mb) * c_exp)
        l_ref[...] += _lane_fold(p, jnp.add)
        acc_ref[...] += _dot(p.astype(BF16), kl)

    def acc_body(j, carry):
        acc_step(j, False)
        return carry

    lax.fori_loop(0, qi, acc_body, 0)
    acc_step(qi, True)
    den = jnp.sum(l_ref[...], axis=-1, keepdims=True)
    o_ref[...] = (acc_ref[...] / den).reshape(h, tq, r).astype(o_ref.dtype)


def attn_prompt(ql, qr, latb, krb, batch, seq, scale, o_buf, *, tq=ATT_BLOCK):
    h, m, r = ql.shape
    rope = qr.shape[2]
    nq = seq // tq
    rows = h * tq
    return pl.pallas_call(
        functools.partial(_attn_prompt_body, c_exp=scale * LOG2E),
        grid=(batch, nq),
        in_specs=[pl.BlockSpec((h, tq, r), lambda b, i: (0, b * nq + i, 0)),
                  pl.BlockSpec((h, tq, rope), lambda b, i: (0, b * nq + i, 0)),
                  pl.BlockSpec((seq, r), lambda b, i: (b, 0)),
                  pl.BlockSpec((seq, rope), lambda b, i: (b, 0)),
                  pl.BlockSpec(memory_space=pl.ANY)],
        out_specs=pl.BlockSpec((h, tq, r), lambda b, i: (0, b * nq + i, 0)),
        out_shape=jax.ShapeDtypeStruct(o_buf.shape, o_buf.dtype),
        scratch_shapes=[pltpu.VMEM((rows, LANE), F32), pltpu.VMEM((rows, LANE), F32),
                        pltpu.VMEM((rows, r), F32)],
        input_output_aliases={4: 0},
        compiler_params=_cparams(("parallel", "parallel"), VMEM_LIMIT),
        name="attn_prompt",
    )(ql, qr, latb, krb, o_buf)


def _attn_sample_body(pt_ref, ql_ref, qr_ref, latn_ref, krn_ref, clat_ref, ckr_ref, oprev_ref,
                      o_ref, lbuf, kbuf, obuf, lsem, ksem, *, layer, scale, n_pages, page):
    del oprev_ref
    b = pl.program_id(0)
    nb = pl.num_programs(0)

    def copies(bb, slot):
        out = []
        for p in range(n_pages):
            pg = pt_ref[bb, p]
            out.append(pltpu.make_async_copy(
                clat_ref.at[pg, :, layer, :], lbuf.at[slot, pl.ds(p * page, page), :], lsem.at[slot]))
            out.append(pltpu.make_async_copy(
                ckr_ref.at[pg, layer], kbuf.at[slot, :, pl.ds(p * page, page)], ksem.at[slot]))
        return out

    @pl.when(b == 0)
    def _():
        for cp in copies(0, 0):
            cp.start()

    slot = b % 2

    @pl.when(b + 1 < nb)
    def _():
        for cp in copies(b + 1, 1 - slot):
            cp.start()

    for cp in copies(b, slot):
        cp.wait()

    ql = ql_ref[...]
    qr = qr_ref[...]
    kl = lbuf[slot].astype(BF16)
    kr_t = kbuf[slot].astype(BF16)
    ln = latn_ref[...]
    kn = krn_ref[...]
    lnb = ln.astype(BF16).astype(F32)
    knb = kn.astype(BF16).astype(F32)
    s_past = (_dot_nt(ql, kl) + _dot(qr, kr_t)) * scale
    s_new = (jnp.sum(ql.astype(F32) * lnb, axis=-1, keepdims=True)
             + jnp.sum(qr.astype(F32) * knb, axis=-1, keepdims=True)) * scale
    m = jnp.maximum(jnp.max(s_past, axis=-1, keepdims=True), s_new)
    p_past = jnp.exp(s_past - m)
    p_new = jnp.exp(s_new - m)
    den = jnp.sum(p_past, axis=-1, keepdims=True) + p_new
    num = _dot(p_past.astype(BF16), kl) + p_new.astype(BF16).astype(F32) * lnb
    obuf[:, pl.ds(b, 1), :] = (num / den)[:, None, :]

    @pl.when(b == nb - 1)
    def _():
        o_ref[...] = obuf[...].astype(o_ref.dtype)


def attn_sample(page_table, ql_s, qr_s, lat_new, kr_new, cache_lat, cache_kr_t, o_all,
                layer, scale, row_block):
    nb, h, r = ql_s.shape
    rope = qr_s.shape[2]
    n_pages = page_table.shape[1]
    page = cache_lat.shape[1]
    t = n_pages * page
    grid_spec = pltpu.PrefetchScalarGridSpec(
        num_scalar_prefetch=1,
        grid=(nb,),
        in_specs=[pl.BlockSpec((None, h, r), lambda b, pt: (b, 0, 0)),
                  pl.BlockSpec((None, h, rope), lambda b, pt: (b, 0, 0)),
                  pl.BlockSpec((None, 1, r), lambda b, pt: (b, 0, 0)),
                  pl.BlockSpec((None, 1, rope), lambda b, pt: (b, 0, 0)),
                  pl.BlockSpec(memory_space=pl.ANY),
                  pl.BlockSpec(memory_space=pl.ANY),
                  pl.BlockSpec(memory_space=pl.ANY)],
        out_specs=pl.BlockSpec((h, nb, r), lambda b, pt: (0, row_block, 0)),
        scratch_shapes=[pltpu.VMEM((2, t, r), F32), pltpu.VMEM((2, rope, t), F32),
                        pltpu.VMEM((h, nb, r), F32),
                        pltpu.SemaphoreType.DMA((2,)), pltpu.SemaphoreType.DMA((2,))],
    )
    return pl.pallas_call(
        functools.partial(_attn_sample_body, layer=layer, scale=scale, n_pages=n_pages, page=page),
        grid_spec=grid_spec,
        out_shape=jax.ShapeDtypeStruct(o_all.shape, o_all.dtype),
        input_output_aliases={7: 0},
        compiler_params=_cparams(("arbitrary",), VMEM_LIMIT),
        name="attn_sample",
    )(page_table, ql_s, qr_s, lat_new, kr_new, cache_lat, cache_kr_t, o_all)


def _rope_tables(positions, reps):
    half = MLA_ROPE // 2
    inv = ROPE_THETA ** (-jnp.arange(half, dtype=F32) / half)
    ang = positions.astype(F32)[:, None] * inv[None, :]
    cos = jnp.cos(ang)
    sin = jnp.sin(ang)
    cos = jnp.concatenate([cos, cos], axis=1)
    sin = jnp.concatenate([-sin, sin], axis=1)
    return jnp.tile(cos, (1, reps)), jnp.tile(sin, (1, reps))


def kernel(x_prompt, x_sample, cache_mla_latent, cache_mla_krope, state_ssd, state_ssd_conv, state_short_conv, page_table, norm_mix_w, w_in, gate_b, sc_conv_w, sc_w_out, ssd_conv_w, ssd_conv_b, ssd_dt_bias, ssd_A_log, ssd_D, ssd_norm_w, ssd_w_out, mla_q_norm_w, mla_w_q_up, mla_kv_norm_w, mla_w_uk, mla_w_uv, mla_w_o, w_out, norm_mlp_w, mlp_w1, mlp_w2, final_norm_w):
    bp, sp, d = x_prompt.shape
    bs, ss, _ = x_sample.shape
    assert ss == 1
    depth = w_in.shape[0]
    mp = bp * sp
    m = mp + bs
    n_pages = page_table.shape[1]
    page = cache_mla_latent.shape[1]
    past_len = n_pages * page

    sc_dim = sc_conv_w.shape[2]
    ssd_dim = ssd_norm_w.shape[1]
    conv_dim = ssd_conv_w.shape[2]
    n_heads = ssd_dt_bias.shape[1]
    q_lora = mla_q_norm_w.shape[1]
    kv_lora = mla_kv_norm_w.shape[1]
    scale = (MLA_NOPE + MLA_ROPE) ** -0.5

    sizes = (sc_dim, sc_dim, sc_dim, ssd_dim, conv_dim, n_heads, q_lora, kv_lora, MLA_ROPE, 3 * d)
    starts = [0]
    for s in sizes:
        starts.append(starts[-1] + s)
    (o_scb, o_scc, o_scx, o_z, o_xbc, o_dt, o_cq, o_ckv, o_kr, o_gate, o_end) = starts
    w_in_t = jnp.swapaxes(w_in, 1, 2)
    n_proj1 = o_dt
    c_z, c_xbc = o_z, o_xbc
    assert n_proj1 % 512 == 0 and (3 * d) % 512 == 0
    small_w = q_lora + kv_lora + 2 * LANE
    c_dt = q_lora + kv_lora + LANE
    assert small_w % 512 == 0
    w_small = jnp.concatenate(
        [w_in[:, :, o_cq:o_gate], jnp.zeros((depth, d, LANE - MLA_ROPE), F32),
         w_in[:, :, o_dt:o_cq], jnp.zeros((depth, d, LANE - n_heads), F32)], axis=2).astype(BF16)
    wq = mla_w_q_up.reshape(depth, q_lora, MLA_HEADS, MLA_NOPE + MLA_ROPE)
    wq = jnp.concatenate([wq[..., :MLA_NOPE].reshape(depth, q_lora, -1),
                          wq[..., MLA_NOPE:].reshape(depth, q_lora, -1)], axis=2).astype(BF16)
    w_uk_t = jnp.transpose(mla_w_uk, (0, 2, 3, 1)).astype(BF16)
    w_uv_h = jnp.transpose(mla_w_uv, (0, 2, 1, 3)).astype(BF16)
    cache_kr_t = jnp.transpose(cache_mla_krope, (0, 2, 3, 1))
    w2_bf16 = mlp_w2.astype(BF16)

    br = 128
    assert sp % br == 0 and bs == br and mp % br == 0
    positions = jnp.concatenate([jnp.arange(sp), jnp.full((br,), past_len)])
    cos_q, sin_q = _rope_tables(positions, MLA_HEADS)
    cos_k, sin_k = _rope_tables(positions, 1)
    zpad = jnp.zeros((positions.shape[0], LANE - MLA_ROPE), F32)
    cos_k = jnp.concatenate([cos_k, zpad], axis=1)
    sin_k = jnp.concatenate([sin_k, zpad], axis=1)
    n_pblk = mp // br
    tbl_index = lambda i: jnp.where(i < n_pblk, i % (sp // br), sp // br)

    sample_blk = mp // bs
    x = jnp.concatenate([x_prompt.reshape(mp, d), x_sample.reshape(bs, d)], axis=0)
    state_all = state_ssd.reshape(depth, bs, ssd_dim, SSD_STATE)
    gn = SSD_GROUPS * SSD_STATE

    lat_l, kr_l, sc_p, sc_s, cv_p, cv_s, st_p = [], [], [], [], [], [], []
    st_new_all = jnp.zeros(state_all.shape, F32)
    u_sc = jnp.zeros((m, sc_dim), BF16)
    u_ssd = jnp.zeros((m, ssd_dim), BF16)
    o_lat = jnp.zeros((MLA_HEADS, m, kv_lora), BF16)
    for l in range(depth):
        h = rmsnorm_rows(x, norm_mix_w[l], BF16)
        proj = matmul_w(h, w_in_t, l, F32, ncols=n_proj1, w_is_nk=True)
        gates = matmul_w(h, w_in_t, l, F32, col0=o_gate, ncols=3 * d, w_is_nk=True)
        proj_s = matmul_w(h, w_small, l, F32)

        u_sc, sc_tail = sc_conv_prompt(proj, sc_conv_w[l], bp, sp, u_sc)
        u_sc, sc_nbuf = sc_conv_sample(proj, jnp.swapaxes(state_short_conv[l], 0, 1),
                                       sc_conv_w[l], u_sc, sample_blk, bs)
        sc_p.append(sc_tail)
        sc_s.append(jnp.swapaxes(sc_nbuf, 0, 1))

        xbc_p, cv_tail = ssd_conv_prompt(proj, c_xbc, ssd_conv_w[l], ssd_conv_b[l], bp, sp)
        u_ssd, st_new_p = ssd_prompt(xbc_p, proj, c_z, proj_s, c_dt, ssd_dt_bias[l], ssd_A_log[l],
                                     ssd_D[l], ssd_norm_w[l], bp, sp, u_ssd)
        act_s, cv_nbuf, da_t, dtx_t = ssd_sample_pre(
            proj, c_xbc, proj_s, c_dt, jnp.swapaxes(state_ssd_conv[l], 0, 1), ssd_conv_w[l],
            ssd_conv_b[l], ssd_dt_bias[l], ssd_A_log[l], sample_blk, bs, ssd_dim)
        st_new_all, y_t = ssd_sample_step(
            state_all, l, da_t, dtx_t,
            act_s[:, ssd_dim:ssd_dim + gn].reshape(bs, SSD_GROUPS, SSD_STATE),
            act_s[:, ssd_dim + gn:].reshape(bs, SSD_GROUPS, SSD_STATE), st_new_all)
        u_ssd = ssd_sample_post(y_t, act_s, proj, c_z, jnp.repeat(ssd_D[l], SSD_HEADDIM),
                                ssd_norm_w[l], u_ssd, sample_blk, bs)
        cv_p.append(cv_tail)
        cv_s.append(jnp.swapaxes(cv_nbuf, 0, 1))
        st_p.append(st_new_p.reshape(bp, n_heads, SSD_HEADDIM, SSD_STATE))

        cqn, lat, latb, kr, krb = mla_pre(proj_s, 0, small_w, mla_q_norm_w[l],
                                          mla_kv_norm_w[l], cos_k, sin_k, tbl_index)
        q = matmul_w(cqn, wq, l, F32, bn=1024)
        ql = q_latent(q, w_uk_t, l)
        qr = q_rope(q, MLA_HEADS * MLA_NOPE, cos_q, sin_q, tbl_index)
        o_lat = attn_prompt(ql, qr, latb, krb, bp, sp, scale, o_lat)
        o_lat = attn_sample(page_table, jnp.swapaxes(ql[:, mp:], 0, 1),
                            jnp.swapaxes(qr[:, mp:], 0, 1), lat[mp:, None, :], kr[mp:, None, :],
                            cache_mla_latent, cache_kr_t, o_lat, l, scale, sample_blk)
        o_mla = v_up(o_lat, w_uv_h, l)
        lat_l.append(lat)
        kr_l.append(kr)

        merged = branch_merge(u_sc, u_ssd, o_mla, sc_w_out, ssd_w_out, mla_w_o, l,
                              gates, 0, gate_b)
        x = matmul_w(merged, w_out, l, F32, res=x, bn=256)
        h2 = rmsnorm_rows(x, norm_mlp_w[l], BF16)
        a = matmul_w(h2, mlp_w1, l, BF16, act="relu2")
        x = matmul_residual(a, w2_bf16, l, x)

    y_prompt = rmsnorm_rows(x, final_norm_w, F32, row0=0, rows=mp).reshape(bp, sp, d)
    y_sample = rmsnorm_rows(x, final_norm_w, F32, row0=mp, rows=bs).reshape(bs, ss, d)
    lat_all = jnp.stack(lat_l, axis=1)
    kr_all = jnp.stack(kr_l, axis=1)
    return (y_prompt, y_sample,
            lat_all[:mp].reshape(bp, sp, depth, kv_lora), kr_all[:mp].reshape(bp, sp, depth, MLA_ROPE),
            lat_all[mp:].reshape(bs, ss, depth, kv_lora), kr_all[mp:].reshape(bs, ss, depth, MLA_ROPE),
            jnp.stack(st_p, axis=0), st_new_all.reshape(depth, bs, n_heads, SSD_HEADDIM, SSD_STATE),
            jnp.stack(cv_p, axis=0), jnp.stack(cv_s, axis=0),
            jnp.stack(sc_p, axis=0), jnp.stack(sc_s, axis=0))
```

```python
import functools
import math

import jax
import jax.numpy as jnp
from jax import lax
from jax.experimental import pallas as pl
from jax.experimental.pallas import tpu as pltpu

F32 = jnp.float32
BF16 = jnp.bfloat16

SC_WIDTH = 3
SSD_HEADDIM = 64
SSD_GROUPS = 8
SSD_STATE = 128
SSD_CONV = 4
SSD_CHUNK = 128
MLA_HEADS = 16
MLA_NOPE = 128
MLA_ROPE = 64
MLA_V = 128
ROPE_THETA = 10000.0
ATT_BLOCK = 256
EPS = 1e-6

LANE = 128
VMEM_LIMIT = 60000 * 1024
LOG2E = 1.4426950408889634


def _cparams(sem, vmem=None):
    return pltpu.CompilerParams(dimension_semantics=sem, vmem_limit_bytes=vmem)


def _row_block(m, target, mult=16):
    best = None
    for d in range(mult, min(m, target) + 1, mult):
        if m % d == 0:
            best = d
    assert best is not None, (m, target)
    return best


def _once(block_shape, index_map):
    return pl.BlockSpec(block_shape, index_map, pipeline_mode=pl.Buffered(1))


def _sigmoid(x):
    return 1.0 / (1.0 + jnp.exp(-x))


def _silu(x):
    return x * _sigmoid(x)


def _softplus(x):
    return jnp.maximum(x, 0.0) + jnp.log1p(jnp.exp(-jnp.abs(x)))


def _dot(a, b):
    return jnp.dot(a, b, preferred_element_type=F32)


def _dot_nt(a, b):
    return lax.dot_general(a, b, (((1,), (1,)), ((), ())), preferred_element_type=F32)


def _rmsnorm_body(x_ref, w_ref, o_ref):
    x = x_ref[...].astype(F32)
    ms = jnp.mean(x * x, axis=-1, keepdims=True)
    o_ref[...] = (x * lax.rsqrt(ms + EPS) * w_ref[...]).astype(o_ref.dtype)


def rmsnorm_rows(x, w, out_dtype, *, row0=0, rows=None, br_target=512):
    m, c = x.shape
    rows = m if rows is None else rows
    br = _row_block(math.gcd(rows, row0) if row0 else rows, br_target)
    off = row0 // br
    return pl.pallas_call(
        _rmsnorm_body,
        grid=(rows // br,),
        in_specs=[pl.BlockSpec((br, c), lambda i: (i + off, 0)),
                  pl.BlockSpec((1, c), lambda i: (0, 0))],
        out_specs=pl.BlockSpec((br, c), lambda i: (i, 0)),
        out_shape=jax.ShapeDtypeStruct((rows, c), out_dtype),
        compiler_params=_cparams(("parallel",), VMEM_LIMIT),
        name="rmsnorm_rows",
    )(x, w.reshape(1, c))


def _mm_body(x_ref, w_ref, *rest, act, w_is_nk):
    o_ref = rest[-1]
    if w_is_nk:
        acc = _dot_nt(x_ref[...], w_ref[0].astype(BF16))
    else:
        acc = _dot(x_ref[...], w_ref[...].astype(BF16))
    if act == "relu2":
        r = jnp.maximum(acc, 0.0)
        acc = r * r
    if len(rest) == 2:
        acc = rest[0][...] + acc
    o_ref[...] = acc.astype(o_ref.dtype)


def matmul_w(x, w, layer, out_dtype, *, col0=0, ncols=None, act=None, w_is_nk=False, res=None,
             bm_target=2080, bn=512):
    m, k = x.shape
    n_total = w.shape[1] if w_is_nk else w.shape[2]
    ncols = n_total - col0 if ncols is None else ncols
    bm = _row_block(m, bm_target)
    assert ncols % bn == 0
    if w_is_nk:
        assert col0 % 8 == 0
        w_spec = pl.BlockSpec((pl.Element(1), pl.Element(bn), pl.Element(k)),
                              lambda i, j: (layer, pl.multiple_of(col0 + j * bn, 8), 0))
    else:
        assert col0 % bn == 0
        w_spec = pl.BlockSpec((None, k, bn), lambda i, j: (layer, 0, col0 // bn + j))
    in_specs = [_once((bm, k), lambda i, j: (i, 0)), w_spec]
    args = [x, w]
    if res is not None:
        in_specs.append(pl.BlockSpec((bm, bn), lambda i, j: (i, j)))
        args.append(res)
    return pl.pallas_call(
        functools.partial(_mm_body, act=act, w_is_nk=w_is_nk),
        grid=(m // bm, ncols // bn),
        in_specs=in_specs,
        out_specs=pl.BlockSpec((bm, bn), lambda i, j: (i, j)),
        out_shape=jax.ShapeDtypeStruct((m, ncols), out_dtype),
        compiler_params=_cparams(("parallel", "arbitrary"), VMEM_LIMIT),
        name="matmul_w",
    )(*args)


def _mm_res_body(x_ref, w_ref, r_ref, o_ref, acc_ref, *, nk):
    kk = pl.program_id(2)

    @pl.when(kk == 0)
    def _():
        acc_ref[...] = jnp.zeros_like(acc_ref)

    acc_ref[...] += _dot(x_ref[...], w_ref[...].astype(BF16))

    @pl.when(kk == nk - 1)
    def _():
        o_ref[...] = r_ref[...] + acc_ref[...]


def matmul_residual(x, w, layer, res, *, bm_target=1040, bn=1024, bk=2048):
    m, k = x.shape
    n = w.shape[2]
    bm = _row_block(m, bm_target)
    assert n % bn == 0 and k % bk == 0
    nk = k // bk
    return pl.pallas_call(
        functools.partial(_mm_res_body, nk=nk),
        grid=(m // bm, n // bn, nk),
        in_specs=[pl.BlockSpec((bm, bk), lambda i, j, kk: (i, kk)),
                  pl.BlockSpec((None, bk, bn), lambda i, j, kk: (layer, kk, j)),
                  pl.BlockSpec((bm, bn), lambda i, j, kk: (i, j))],
        out_specs=pl.BlockSpec((bm, bn), lambda i, j, kk: (i, j)),
        out_shape=jax.ShapeDtypeStruct((m, n), F32),
        scratch_shapes=[pltpu.VMEM((bm, bn), F32)],
        compiler_params=_cparams(("parallel", "parallel", "arbitrary"), VMEM_LIMIT),
        name="matmul_residual",
    )(x, w, res)


def _merge_body(usc_ref, ussd_ref, o_ref, wsc_ref, wssd_ref, wo_ref,
                g0_ref, g1_ref, g2_ref, gb_ref, out_ref):
    def branch(u_ref, w_ref, g_ref, k):
        y = _dot(u_ref[...], w_ref[...].astype(BF16))
        return _sigmoid(g_ref[...] + gb_ref[k:k + 1, :]) * y

    merged = branch(usc_ref, wsc_ref, g0_ref, 0) + branch(ussd_ref, wssd_ref, g1_ref, 1)
    merged = merged + branch(o_ref, wo_ref, g2_ref, 2)
    out_ref[...] = merged.astype(out_ref.dtype)


def branch_merge(u_sc, u_ssd, o_mla, w_sc, w_ssd, w_o, layer, proj, gate_col0, gate_b,
                 *, bm_target=1040, bn=256):
    m, kdim = u_sc.shape
    d = w_sc.shape[2]
    bm = _row_block(m, bm_target)
    goff = gate_col0 // bn
    nd = d // bn
    u_spec = _once((bm, kdim), lambda i, j: (i, 0))
    w_spec = pl.BlockSpec((None, kdim, bn), lambda i, j: (layer, 0, j))

    def g_spec(k):
        return pl.BlockSpec((bm, bn), lambda i, j: (i, goff + k * nd + j))

    return pl.pallas_call(
        _merge_body,
        grid=(m // bm, nd),
        in_specs=[u_spec, u_spec, u_spec, w_spec, w_spec, w_spec,
                  g_spec(0), g_spec(1), g_spec(2),
                  pl.BlockSpec((None, 3, bn), lambda i, j: (layer, 0, j))],
        out_specs=pl.BlockSpec((bm, bn), lambda i, j: (i, j)),
        out_shape=jax.ShapeDtypeStruct((m, d), BF16),
        compiler_params=_cparams(("parallel", "arbitrary"), VMEM_LIMIT),
        name="branch_merge",
    )(u_sc, u_ssd, o_mla, w_sc, w_ssd, w_o, proj, proj, proj, gate_b)


def _q_latent_body(x_ref, w_ref, o_ref):
    h, nope, _ = w_ref.shape
    for hh in range(h):
        xh = x_ref[:, hh * nope:(hh + 1) * nope].astype(BF16)
        o_ref[hh] = _dot(xh, w_ref[hh]).astype(o_ref.dtype)


def q_latent(q, w_uk_t, layer, *, bm_target=1040):
    m = q.shape[0]
    _, h, nope, r = w_uk_t.shape
    bm = _row_block(m, bm_target)
    return pl.pallas_call(
        _q_latent_body,
        grid=(m // bm,),
        in_specs=[pl.BlockSpec((bm, h * nope), lambda i: (i, 0)),
                  pl.BlockSpec((None, h, nope, r), lambda i: (layer, 0, 0, 0))],
        out_specs=pl.BlockSpec((h, bm, r), lambda i: (0, i, 0)),
        out_shape=jax.ShapeDtypeStruct((h, m, r), BF16),
        compiler_params=_cparams(("parallel",), VMEM_LIMIT),
        name="q_latent",
    )(q, w_uk_t)


def _v_up_body(x_ref, w_ref, o_ref):
    h, _, v = w_ref.shape
    for hh in range(h):
        o_ref[:, hh * v:(hh + 1) * v] = _dot(x_ref[hh], w_ref[hh]).astype(o_ref.dtype)


def v_up(o_lat, w_uv_h, layer, *, bm_target=1040):
    h, m, r = o_lat.shape
    v = w_uv_h.shape[3]
    bm = _row_block(m, bm_target)
    return pl.pallas_call(
        _v_up_body,
        grid=(m // bm,),
        in_specs=[pl.BlockSpec((h, bm, r), lambda i: (0, i, 0)),
                  pl.BlockSpec((None, h, r, v), lambda i: (layer, 0, 0, 0))],
        out_specs=pl.BlockSpec((bm, h * v), lambda i: (i, 0)),
        out_shape=jax.ShapeDtypeStruct((m, h * v), BF16),
        compiler_params=_cparams(("parallel",), VMEM_LIMIT),
        name="v_up",
    )(o_lat, w_uv_h)


def _shift_rows(u, k, row):
    return jnp.where(row >= k, pltpu.roll(u, k, 0), 0.0)


def _sc_conv_prompt_body(b_ref, c_ref, x_ref, w_ref, ubuf_ref, u_ref, tail_ref):
    del ubuf_ref
    u = c_ref[...] * x_ref[...]
    seq = u.shape[0]
    row = lax.broadcasted_iota(jnp.int32, u.shape, 0)
    v = _shift_rows(u, 2, row) * w_ref[0:1, :]
    v = v + _shift_rows(u, 1, row) * w_ref[1:2, :]
    v = v + u * w_ref[2:3, :]
    u_ref[...] = (b_ref[...] * v).astype(u_ref.dtype)
    tail_ref[...] = u[seq - (SC_WIDTH - 1):, :]


def sc_conv_prompt(proj, w, batch, seq, u_buf, *, bc=512):
    sc_dim = w.shape[1]
    nc = sc_dim // bc
    return pl.pallas_call(
        _sc_conv_prompt_body,
        grid=(batch, nc),
        in_specs=[pl.BlockSpec((seq, bc), lambda b, j: (b, j)),
                  pl.BlockSpec((seq, bc), lambda b, j: (b, nc + j)),
                  pl.BlockSpec((seq, bc), lambda b, j: (b, 2 * nc + j)),
                  pl.BlockSpec((SC_WIDTH, bc), lambda b, j: (0, j)),
                  pl.BlockSpec(memory_space=pl.ANY)],
        out_specs=[pl.BlockSpec((seq, bc), lambda b, j: (b, j)),
                   pl.BlockSpec((None, SC_WIDTH - 1, bc), lambda b, j: (b, 0, j))],
        out_shape=[jax.ShapeDtypeStruct(u_buf.shape, u_buf.dtype),
                   jax.ShapeDtypeStruct((batch, SC_WIDTH - 1, sc_dim), F32)],
        input_output_aliases={4: 0},
        compiler_params=_cparams(("parallel", "parallel"), VMEM_LIMIT),
        name="sc_conv_prompt",
    )(proj, proj, proj, w, u_buf)


def _sc_conv_sample_body(b_ref, c_ref, x_ref, buf_ref, w_ref, uprev_ref, u_ref, nbuf_ref):
    del uprev_ref
    u = c_ref[...] * x_ref[...]
    v = buf_ref[0] * w_ref[0:1, :]
    v = v + buf_ref[1] * w_ref[1:2, :]
    v = v + u * w_ref[2:3, :]
    u_ref[...] = (b_ref[...] * v).astype(u_ref.dtype)
    nbuf_ref[0] = buf_ref[1]
    nbuf_ref[1] = u


def sc_conv_sample(proj, buf_t, w, u_all, row_block, nb):
    sc_dim = w.shape[1]
    return pl.pallas_call(
        _sc_conv_sample_body,
        grid=(1,),
        in_specs=[pl.BlockSpec((nb, sc_dim), lambda i: (row_block, 0)),
                  pl.BlockSpec((nb, sc_dim), lambda i: (row_block, 1)),
                  pl.BlockSpec((nb, sc_dim), lambda i: (row_block, 2)),
                  pl.BlockSpec((SC_WIDTH - 1, nb, sc_dim), lambda i: (0, 0, 0)),
                  pl.BlockSpec((SC_WIDTH, sc_dim), lambda i: (0, 0)),
                  pl.BlockSpec(memory_space=pl.ANY)],
        out_specs=[pl.BlockSpec((nb, sc_dim), lambda i: (row_block, 0)),
                   pl.BlockSpec((SC_WIDTH - 1, nb, sc_dim), lambda i: (0, 0, 0))],
        out_shape=[jax.ShapeDtypeStruct(u_all.shape, u_all.dtype),
                   jax.ShapeDtypeStruct((SC_WIDTH - 1, nb, sc_dim), F32)],
        input_output_aliases={5: 0},
        compiler_params=_cparams(("arbitrary",)),
        name="sc_conv_sample",
    )(proj, proj, proj, buf_t, w, u_all)


def _ssd_conv_prompt_body(x_ref, w_ref, bias_ref, o_ref, tail_ref):
    u = x_ref[...]
    seq = u.shape[0]
    row = lax.broadcasted_iota(jnp.int32, u.shape, 0)
    v = _shift_rows(u, 3, row) * w_ref[0:1, :]
    v = v + _shift_rows(u, 2, row) * w_ref[1:2, :]
    v = v + _shift_rows(u, 1, row) * w_ref[2:3, :]
    v = v + u * w_ref[3:4, :]
    o_ref[...] = _silu(v + bias_ref[...])
    tail_ref[...] = u[seq - (SSD_CONV - 1):, :]


def ssd_conv_prompt(proj, col0, w, bias, batch, seq, *, bc=512):
    cdim = w.shape[1]
    nc = cdim // bc
    off = col0 // bc
    return pl.pallas_call(
        _ssd_conv_prompt_body,
        grid=(batch, nc),
        in_specs=[pl.BlockSpec((seq, bc), lambda b, j: (b, off + j)),
                  pl.BlockSpec((SSD_CONV, bc), lambda b, j: (0, j)),
                  pl.BlockSpec((1, bc), lambda b, j: (0, j))],
        out_specs=[pl.BlockSpec((seq, bc), lambda b, j: (b, j)),
                   pl.BlockSpec((None, SSD_CONV - 1, bc), lambda b, j: (b, 0, j))],
        out_shape=[jax.ShapeDtypeStruct((batch * seq, cdim), F32),
                   jax.ShapeDtypeStruct((batch, SSD_CONV - 1, cdim), F32)],
        compiler_params=_cparams(("parallel", "parallel"), VMEM_LIMIT),
        name="ssd_conv_prompt",
    )(proj, w, bias.reshape(1, cdim))


def _cumsum_rows(a):
    n = a.shape[0]
    row = lax.broadcasted_iota(jnp.int32, a.shape, 0)
    k = 1
    while k < n:
        a = a + _shift_rows(a, k, row)
        k *= 2
    return a


def _ssd_prompt_body(xs_ref, b_ref, c_ref, z_ref, dt_ref, dtb_ref, alog_ref, d_ref, nw_ref,
                     ubuf_ref, u_ref, st_ref, h_ref, y_ref, *, n_chunks, gw):
    del ubuf_ref
    gb = pl.program_id(1)
    c = pl.program_id(2)
    gpb = y_ref.shape[1] // gw
    e_per_g = gw // SSD_HEADDIM
    q = xs_ref.shape[0]
    n = SSD_STATE

    @pl.when(c == 0)
    def _():
        h_ref[...] = jnp.zeros_like(h_ref)

    x = xs_ref[...]
    dt_all = _softplus(dt_ref[...] + dtb_ref[...])
    acum_all = _cumsum_rows(dt_all * (-jnp.exp(alog_ref[...])))
    shift = (LANE - gpb * e_per_g * gb) % LANE
    dt_g = pltpu.roll(dt_all, shift, 1)
    ac_g = pltpu.roll(acum_all, shift, 1)
    dt_t = dt_g.T
    ac_t = ac_g.T
    x_t = x.T
    tri = (lax.broadcasted_iota(jnp.int32, (q, q), 0)
           >= lax.broadcasted_iota(jnp.int32, (q, q), 1))
    for gi in range(gpb):
        bmat = b_ref[:, gi * n:(gi + 1) * n].astype(BF16)
        cmat = c_ref[:, gi * n:(gi + 1) * n].astype(BF16)
        cb = _dot_nt(cmat, bmat)
        for e in range(e_per_g):
            hd = gi * e_per_g + e
            lo, hi = hd * SSD_HEADDIM, (hd + 1) * SSD_HEADDIM
            a_col = ac_g[:, hd:hd + 1]
            a_row = ac_t[hd:hd + 1, :]
            dt_row = dt_t[hd:hd + 1, :]
            a_last = ac_g[q - 1:q, hd:hd + 1]
            decay = jnp.exp(jnp.where(tri, a_col - a_row, -jnp.inf))
            w_ts = cb * decay * dt_row
            xe = x[:, lo:hi]
            he = h_ref[lo:hi, :]
            y = _dot(w_ts.astype(BF16), xe.astype(BF16))
            y = y + _dot_nt(cmat, he.astype(BF16)) * jnp.exp(a_col)
            y_ref[:, lo:hi] = y + xe * d_ref[:, lo:hi]
            to_end = jnp.exp(a_last - a_row) * dt_row
            s_chunk = _dot((x_t[lo:hi, :] * to_end).astype(BF16), bmat)
            h_ref[lo:hi, :] = he * jnp.exp(a_last) + s_chunk

    for gi in range(gpb):
        sl = slice(gi * gw, (gi + 1) * gw)
        yz = y_ref[:, sl] * _silu(z_ref[:, sl])
        ms = jnp.mean(yz * yz, axis=-1, keepdims=True)
        u_ref[:, sl] = (yz * lax.rsqrt(ms + EPS) * nw_ref[:, sl]).astype(u_ref.dtype)

    @pl.when(c == n_chunks - 1)
    def _():
        for gi in range(gpb):
            st_ref[gi] = h_ref[gi * gw:(gi + 1) * gw, :]


def ssd_prompt(xbc, proj, z_col0, proj_dt, dt_col0, dt_bias, a_log, d_skip, norm_w, batch, seq,
               u_buf):
    ssd_dim = norm_w.shape[0]
    gw = ssd_dim // SSD_GROUPS
    gpb = 2
    bw = gpb * gw
    bn_ = gpb * SSD_STATE
    q = SSD_CHUNK
    n_chunks = seq // q
    assert SSD_GROUPS % gpb == 0 and ssd_dim % bn_ == 0 and z_col0 % bw == 0
    b_blk0 = ssd_dim // bn_
    c_blk0 = b_blk0 + SSD_GROUPS // gpb
    pad = lambda v: jnp.zeros((1, LANE), F32).at[0, :v.shape[0]].set(v)
    row = lambda b, c: b * n_chunks + c
    return pl.pallas_call(
        functools.partial(_ssd_prompt_body, n_chunks=n_chunks, gw=gw),
        grid=(batch, SSD_GROUPS // gpb, n_chunks),
        in_specs=[pl.BlockSpec((q, bw), lambda b, g, c: (row(b, c), g)),
                  pl.BlockSpec((q, bn_), lambda b, g, c: (row(b, c), b_blk0 + g)),
                  pl.BlockSpec((q, bn_), lambda b, g, c: (row(b, c), c_blk0 + g)),
                  pl.BlockSpec((q, bw), lambda b, g, c: (row(b, c), z_col0 // bw + g)),
                  pl.BlockSpec((q, LANE), lambda b, g, c: (row(b, c), dt_col0 // LANE)),
                  pl.BlockSpec((1, LANE), lambda b, g, c: (0, 0)),
                  pl.BlockSpec((1, LANE), lambda b, g, c: (0, 0)),
                  pl.BlockSpec((1, bw), lambda b, g, c: (0, g)),
                  pl.BlockSpec((1, bw), lambda b, g, c: (0, g)),
                  pl.BlockSpec(memory_space=pl.ANY)],
        out_specs=[pl.BlockSpec((q, bw), lambda b, g, c: (row(b, c), g)),
                   pl.BlockSpec((None, gpb, gw, SSD_STATE), lambda b, g, c: (b, g, 0, 0))],
        out_shape=[jax.ShapeDtypeStruct(u_buf.shape, u_buf.dtype),
                   jax.ShapeDtypeStruct((batch, SSD_GROUPS, gw, SSD_STATE), F32)],
        scratch_shapes=[pltpu.VMEM((bw, SSD_STATE), F32), pltpu.VMEM((q, bw), F32)],
        input_output_aliases={9: 0},
        compiler_params=_cparams(("parallel", "parallel", "arbitrary")),
        name="ssd_prompt",
    )(xbc, xbc, xbc, proj, proj_dt, pad(dt_bias), pad(a_log),
      jnp.repeat(d_skip, SSD_HEADDIM).reshape(1, ssd_dim), norm_w.reshape(1, ssd_dim), u_buf)


def _ssd_sample_pre_body(x_ref, buf_ref, w_ref, bias_ref, dt_ref, dtb_ref, alog_ref,
                         act_ref, nbuf_ref, da_t_ref, dtx_t_ref, *, ssd_dim, n_heads):
    u = x_ref[...]
    v = buf_ref[0] * w_ref[0:1, :]
    v = v + buf_ref[1] * w_ref[1:2, :]
    v = v + buf_ref[2] * w_ref[2:3, :]
    v = v + u * w_ref[3:4, :]
    act = _silu(v + bias_ref[...])
    act_ref[...] = act
    nbuf_ref[0] = buf_ref[1]
    nbuf_ref[1] = buf_ref[2]
    nbuf_ref[2] = u
    dt = _softplus(dt_ref[...] + dtb_ref[...])
    da = jnp.exp(dt * (-jnp.exp(alog_ref[...])))
    dt_t = dt.T
    da_t = da.T
    x_t = act[:, :ssd_dim].T
    nb = u.shape[0]
    for h in range(n_heads):
        lo, hi = h * SSD_HEADDIM, (h + 1) * SSD_HEADDIM
        da_t_ref[lo:hi, :] = jnp.broadcast_to(da_t[h:h + 1, :], (SSD_HEADDIM, nb))
        dtx_t_ref[lo:hi, :] = dt_t[h:h + 1, :] * x_t[lo:hi, :]


def ssd_sample_pre(proj, xbc_col0, proj_dt, dt_col0, buf_t, w, bias, dt_bias, a_log,
                   row_block, nb, ssd_dim):
    cdim = w.shape[1]
    n_heads = ssd_dim // SSD_HEADDIM
    pad = lambda v: jnp.zeros((1, LANE), F32).at[0, :v.shape[0]].set(v)
    full = lambda *s: pl.BlockSpec(s, lambda i: (0,) * len(s))
    return pl.pallas_call(
        functools.partial(_ssd_sample_pre_body, ssd_dim=ssd_dim, n_heads=n_heads),
        grid=(1,),
        in_specs=[pl.BlockSpec((nb, cdim), lambda i: (row_block, xbc_col0 // cdim)),
                  full(SSD_CONV - 1, nb, cdim), full(SSD_CONV, cdim), full(1, cdim),
                  pl.BlockSpec((nb, LANE), lambda i: (row_block, dt_col0 // LANE)),
                  full(1, LANE), full(1, LANE)],
        out_specs=[full(nb, cdim), full(SSD_CONV - 1, nb, cdim), full(ssd_dim, nb), full(ssd_dim, nb)],
        out_shape=[jax.ShapeDtypeStruct((nb, cdim), F32),
                   jax.ShapeDtypeStruct((SSD_CONV - 1, nb, cdim), F32),
                   jax.ShapeDtypeStruct((ssd_dim, nb), F32),
                   jax.ShapeDtypeStruct((ssd_dim, nb), F32)],
        compiler_params=_cparams(("arbitrary",), VMEM_LIMIT),
        name="ssd_sample_pre",
    )(proj, buf_t, w, bias.reshape(1, cdim), proj_dt, pad(dt_bias), pad(a_log))


def _ssd_sample_step_body(st_ref, da_t_ref, dtx_t_ref, b_ref, c_ref, nbuf_ref, nst_ref, y_t_ref):
    del nbuf_ref
    b = pl.program_id(0)

    @pl.when(b == 0)
    def _():
        y_t_ref[...] = jnp.zeros_like(y_t_ref)

    rows, nb = da_t_ref.shape
    groups, n = b_ref.shape
    lane = lax.broadcasted_iota(jnp.int32, (rows, nb), 1)
    pick = lambda t: jnp.sum(jnp.where(lane == b, t, 0.0), axis=1, keepdims=True)
    da = pick(da_t_ref[...])
    dtx = pick(dtx_t_ref[...])
    rep = lambda m: jnp.broadcast_to(m[:, None, :], (groups, rows // groups, n)).reshape(rows, n)
    h_new = st_ref[...] * da + dtx * rep(b_ref[...])
    nst_ref[...] = h_new
    y = jnp.sum(h_new * rep(c_ref[...]), axis=1, keepdims=True)
    y_t_ref[...] = jnp.where(lane == b, y, y_t_ref[...])


def ssd_sample_step(state_all, layer, da_t, dtx_t, bmat, cmat, new_all):
    depth, nb, rows, n = state_all.shape
    groups = bmat.shape[1]
    return pl.pallas_call(
        _ssd_sample_step_body,
        grid=(nb,),
        in_specs=[pl.BlockSpec((None, None, rows, n), lambda b: (layer, b, 0, 0)),
                  pl.BlockSpec((rows, nb), lambda b: (0, 0)),
                  pl.BlockSpec((rows, nb), lambda b: (0, 0)),
                  pl.BlockSpec((None, groups, n), lambda b: (b, 0, 0)),
                  pl.BlockSpec((None, groups, n), lambda b: (b, 0, 0)),
                  pl.BlockSpec(memory_space=pl.ANY)],
        out_specs=[pl.BlockSpec((None, None, rows, n), lambda b: (layer, b, 0, 0)),
                   pl.BlockSpec((rows, nb), lambda b: (0, 0))],
        out_shape=[jax.ShapeDtypeStruct((depth, nb, rows, n), F32),
                   jax.ShapeDtypeStruct((rows, nb), F32)],
        input_output_aliases={5: 0},
        compiler_params=_cparams(("arbitrary",)),
        name="ssd_sample_step",
    )(state_all, da_t, dtx_t, bmat, cmat, new_all)


def _ssd_sample_post_body(y_t_ref, x_ref, z_ref, d_ref, nw_ref, uprev_ref, u_ref, *, gw):
    del uprev_ref
    y = y_t_ref[...].T + x_ref[...] * d_ref[...]
    yz = y * _silu(z_ref[...])
    for g in range(yz.shape[1] // gw):
        blk = yz[:, g * gw:(g + 1) * gw]
        ms = jnp.mean(blk * blk, axis=-1, keepdims=True)
        u_ref[:, g * gw:(g + 1) * gw] = (
            blk * lax.rsqrt(ms + EPS) * nw_ref[:, g * gw:(g + 1) * gw]).astype(u_ref.dtype)


def ssd_sample_post(y_t, act, proj, z_col0, d_rep, norm_w, u_all, row_block, nb):
    ssd_dim = norm_w.shape[0]
    gw = ssd_dim // SSD_GROUPS
    full = lambda *s: pl.BlockSpec(s, lambda i: (0,) * len(s))
    return pl.pallas_call(
        functools.partial(_ssd_sample_post_body, gw=gw),
        grid=(1,),
        in_specs=[full(ssd_dim, nb),
                  pl.BlockSpec((nb, ssd_dim), lambda i: (0, 0)),
                  pl.BlockSpec((nb, ssd_dim), lambda i: (row_block, z_col0 // ssd_dim)),
                  full(1, ssd_dim), full(1, ssd_dim),
                  pl.BlockSpec(memory_space=pl.ANY)],
        out_specs=pl.BlockSpec((nb, ssd_dim), lambda i: (row_block, 0)),
        out_shape=jax.ShapeDtypeStruct(u_all.shape, u_all.dtype),
        input_output_aliases={5: 0},
        compiler_params=_cparams(("arbitrary",)),
        name="ssd_sample_post",
    )(y_t, act, proj, d_rep.reshape(1, ssd_dim), norm_w.reshape(1, ssd_dim), u_all)


def _rope_pairs(x, cos, sin_signed, half, lane):
    width = x.shape[1]
    first = (lane % (2 * half)) < half
    partner = jnp.where(first, pltpu.roll(x, width - half, 1), pltpu.roll(x, half, 1))
    return x * cos + partner * sin_signed


def _mla_pre_body(p_ref, qw_ref, kvw_ref, cos_ref, sin_ref,
                  cqn_ref, lat_ref, latb_ref, kr_ref, krb_ref, *, q_lora, kv_lora):
    cq = p_ref[:, :q_lora]
    ms = jnp.mean(cq * cq, axis=-1, keepdims=True)
    cqn_ref[...] = (cq * lax.rsqrt(ms + EPS) * qw_ref[...]).astype(cqn_ref.dtype)
    ckv = p_ref[:, q_lora:q_lora + kv_lora]
    ms = jnp.mean(ckv * ckv, axis=-1, keepdims=True)
    lat = ckv * lax.rsqrt(ms + EPS) * kvw_ref[...]
    lat_ref[...] = lat
    latb_ref[...] = lat.astype(BF16)
    kr = p_ref[:, q_lora + kv_lora:q_lora + kv_lora + LANE]
    lane = lax.broadcasted_iota(jnp.int32, kr.shape, 1)
    rot = _rope_pairs(kr, cos_ref[...], sin_ref[...], MLA_ROPE // 2, lane)[:, :MLA_ROPE]
    kr_ref[...] = rot
    krb_ref[...] = rot.astype(BF16)


def mla_pre(proj, col0, width, q_norm_w, kv_norm_w, cos_k, sin_k, tbl_index, *, br=128):
    m = proj.shape[0]
    q_lora, kv_lora = q_norm_w.shape[0], kv_norm_w.shape[0]
    row = lambda i: (i, 0)
    return pl.pallas_call(
        functools.partial(_mla_pre_body, q_lora=q_lora, kv_lora=kv_lora),
        grid=(m // br,),
        in_specs=[pl.BlockSpec((br, width), lambda i: (i, col0 // width)),
                  pl.BlockSpec((1, q_lora), lambda i: (0, 0)),
                  pl.BlockSpec((1, kv_lora), lambda i: (0, 0)),
                  pl.BlockSpec((br, LANE), lambda i: (tbl_index(i), 0)),
                  pl.BlockSpec((br, LANE), lambda i: (tbl_index(i), 0))],
        out_specs=[pl.BlockSpec((br, q_lora), row), pl.BlockSpec((br, kv_lora), row),
                   pl.BlockSpec((br, kv_lora), row), pl.BlockSpec((br, MLA_ROPE), row),
                   pl.BlockSpec((br, MLA_ROPE), row)],
        out_shape=[jax.ShapeDtypeStruct((m, q_lora), BF16),
                   jax.ShapeDtypeStruct((m, kv_lora), F32),
                   jax.ShapeDtypeStruct((m, kv_lora), BF16),
                   jax.ShapeDtypeStruct((m, MLA_ROPE), F32),
                   jax.ShapeDtypeStruct((m, MLA_ROPE), BF16)],
        compiler_params=_cparams(("parallel",)),
        name="mla_pre",
    )(proj, q_norm_w.reshape(1, q_lora), kv_norm_w.reshape(1, kv_lora), cos_k, sin_k)


def _q_rope_body(q_ref, cos_ref, sin_ref, o_ref):
    x = q_ref[...]
    lane = lax.broadcasted_iota(jnp.int32, x.shape, 1)
    rot = _rope_pairs(x, cos_ref[...], sin_ref[...], MLA_ROPE // 2, lane)
    for h in range(o_ref.shape[0]):
        o_ref[h] = rot[:, h * MLA_ROPE:(h + 1) * MLA_ROPE].astype(o_ref.dtype)


def q_rope(q, col0, cos_q, sin_q, tbl_index, *, br=128):
    m = q.shape[0]
    width = MLA_HEADS * MLA_ROPE
    return pl.pallas_call(
        _q_rope_body,
        grid=(m // br,),
        in_specs=[pl.BlockSpec((br, width), lambda i: (i, col0 // width)),
                  pl.BlockSpec((br, width), lambda i: (tbl_index(i), 0)),
                  pl.BlockSpec((br, width), lambda i: (tbl_index(i), 0))],
        out_specs=pl.BlockSpec((MLA_HEADS, br, MLA_ROPE), lambda i: (0, i, 0)),
        out_shape=jax.ShapeDtypeStruct((MLA_HEADS, m, MLA_ROPE), BF16),
        compiler_params=_cparams(("parallel",)),
        name="q_rope",
    )(q, cos_q, sin_q)


def _lane_fold(s, op):
    out = s[:, :LANE]
    for c in range(1, s.shape[1] // LANE):
        out = op(out, s[:, c * LANE:(c + 1) * LANE])
    return out


def _attn_prompt_body(ql_ref, qr_ref, lat_ref, kr_ref, obuf_ref, o_ref, m_ref, l_ref, acc_ref,
                      *, c_exp):
    del obuf_ref
    qi = pl.program_id(1)
    h, tq, r = ql_ref.shape
    rows = h * tq
    ql = ql_ref[...].reshape(rows, r)
    qr = qr_ref[...].reshape(rows, qr_ref.shape[2])

    def scores(j, masked):
        k0 = pl.multiple_of(j * tq, tq)
        kl = lat_ref[pl.ds(k0, tq), :]
        kr = kr_ref[pl.ds(k0, tq), :]
        s = _dot_nt(ql, kl) + _dot_nt(qr, kr)
        if masked:
            s3 = s.reshape(h, tq, tq)
            ok = (lax.broadcasted_iota(jnp.int32, s3.shape, 2)
                  <= lax.broadcasted_iota(jnp.int32, s3.shape, 1))
            s = jnp.where(ok, s3, -jnp.inf).reshape(rows, tq)
        return s, kl

    m_ref[...] = jnp.full_like(m_ref, -jnp.inf)

    def max_step(j, masked):
        s, _ = scores(j, masked)
        m_ref[...] = jnp.maximum(m_ref[...], _lane_fold(s, jnp.maximum))

    def max_body(j, carry):
        max_step(j, False)
        return carry

    lax.fori_loop(0, qi, max_body, 0)
    max_step(qi, True)
    m_ref[...] = jnp.broadcast_to(jnp.max(m_ref[...], axis=-1, keepdims=True), m_ref.shape)

    l_ref[...] = jnp.zeros_like(l_ref)
    acc_ref[...] = jnp.zeros_like(acc_ref)

    def acc_step(j, masked):
        s, kl = scores(j, masked)
        mb = jnp.concatenate([m_ref[...]] * (tq // LANE), axis=1)
        p = jnp.exp2((s - mb) * c_exp)
        l_ref[...] += _lane_fold(p, jnp.add)
        acc_ref[...] += _dot(p.astype(BF16), kl)

    def acc_body(j, carry):
        acc_step(j, False)
        return carry

    lax.fori_loop(0, qi, acc_body, 0)
    acc_step(qi, True)
    den = jnp.sum(l_ref[...], axis=-1, keepdims=True)
    o_ref[...] = (acc_ref[...] / den).reshape(h, tq, r).astype(o_ref.dtype)


def attn_prompt(ql, qr, latb, krb, batch, seq, scale, o_buf, *, tq=ATT_BLOCK):
    h, m, r = ql.shape
    rope = qr.shape[2]
    nq = seq // tq
    rows = h * tq
    return pl.pallas_call(
        functools.partial(_attn_prompt_body, c_exp=scale * LOG2E),
        grid=(batch, nq),
        in_specs=[pl.BlockSpec((h, tq, r), lambda b, i: (0, b * nq + i, 0)),
                  pl.BlockSpec((h, tq, rope), lambda b, i: (0, b * nq + i, 0)),
                  pl.BlockSpec((seq, r), lambda b, i: (b, 0)),
                  pl.BlockSpec((seq, rope), lambda b, i: (b, 0)),
                  pl.BlockSpec(memory_space=pl.ANY)],
        out_specs=pl.BlockSpec((h, tq, r), lambda b, i: (0, b * nq + i, 0)),
        out_shape=jax.ShapeDtypeStruct(o_buf.shape, o_buf.dtype),
        scratch_shapes=[pltpu.VMEM((rows, LANE), F32), pltpu.VMEM((rows, LANE), F32),
                        pltpu.VMEM((rows, r), F32)],
        input_output_aliases={4: 0},
        compiler_params=_cparams(("parallel", "parallel"), VMEM_LIMIT),
        name="attn_prompt",
    )(ql, qr, latb, krb, o_buf)


def _attn_sample_body(pt_ref, ql_ref, qr_ref, latn_ref, krn_ref, clat_ref, ckr_ref, oprev_ref,
                      o_ref, lbuf, kbuf, obuf, lsem, ksem, *, layer, scale, n_pages, page):
    del oprev_ref
    b = pl.program_id(0)
    nb = pl.num_programs(0)

    def copies(bb, slot):
        out = []
        for p in range(n_pages):
            pg = pt_ref[bb, p]
            out.append(pltpu.make_async_copy(
                clat_ref.at[pg, :, layer, :], lbuf.at[slot, pl.ds(p * page, page), :], lsem.at[slot]))
            out.append(pltpu.make_async_copy(
                ckr_ref.at[pg, layer], kbuf.at[slot, :, pl.ds(p * page, page)], ksem.at[slot]))
        return out

    @pl.when(b == 0)
    def _():
        for cp in copies(0, 0):
            cp.start()

    slot = b % 2

    @pl.when(b + 1 < nb)
    def _():
        for cp in copies(b + 1, 1 - slot):
            cp.start()

    for cp in copies(b, slot):
        cp.wait()

    ql = ql_ref[...]
    qr = qr_ref[...]
    kl = lbuf[slot].astype(BF16)
    kr_t = kbuf[slot].astype(BF16)
    ln = latn_ref[...]
    kn = krn_ref[...]
    lnb = ln.astype(BF16).astype(F32)
    knb = kn.astype(BF16).astype(F32)
    s_past = (_dot_nt(ql, kl) + _dot(qr, kr_t)) * scale
    s_new = (jnp.sum(ql.astype(F32) * lnb, axis=-1, keepdims=True)
             + jnp.sum(qr.astype(F32) * knb, axis=-1, keepdims=True)) * scale
    m = jnp.maximum(jnp.max(s_past, axis=-1, keepdims=True), s_new)
    p_past = jnp.exp(s_past - m)
    p_new = jnp.exp(s_new - m)
    den = jnp.sum(p_past, axis=-1, keepdims=True) + p_new
    num = _dot(p_past.astype(BF16), kl) + p_new.astype(BF16).astype(F32) * lnb
    obuf[:, pl.ds(b, 1), :] = (num / den)[:, None, :]

    @pl.when(b == nb - 1)
    def _():
        o_ref[...] = obuf[...].astype(o_ref.dtype)


def attn_sample(page_table, ql_s, qr_s, lat_new, kr_new, cache_lat, cache_kr_t, o_all,
                layer, scale, row_block):
    nb, h, r = ql_s.shape
    rope = qr_s.shape[2]
    n_pages = page_table.shape[1]
    page = cache_lat.shape[1]
    t = n_pages * page
    grid_spec = pltpu.PrefetchScalarGridSpec(
        num_scalar_prefetch=1,
        grid=(nb,),
        in_specs=[pl.BlockSpec((None, h, r), lambda b, pt: (b, 0, 0)),
                  pl.BlockSpec((None, h, rope), lambda b, pt: (b, 0, 0)),
                  pl.BlockSpec((None, 1, r), lambda b, pt: (b, 0, 0)),
                  pl.BlockSpec((None, 1, rope), lambda b, pt: (b, 0, 0)),
                  pl.BlockSpec(memory_space=pl.ANY),
                  pl.BlockSpec(memory_space=pl.ANY),
                  pl.BlockSpec(memory_space=pl.ANY)],
        out_specs=pl.BlockSpec((h, nb, r), lambda b, pt: (0, row_block, 0)),
        scratch_shapes=[pltpu.VMEM((2, t, r), F32), pltpu.VMEM((2, rope, t), F32),
                        pltpu.VMEM((h, nb, r), F32),
                        pltpu.SemaphoreType.DMA((2,)), pltpu.SemaphoreType.DMA((2,))],
    )
    return pl.pallas_call(
        functools.partial(_attn_sample_body, layer=layer, scale=scale, n_pages=n_pages, page=page),
        grid_spec=grid_spec,
        out_shape=jax.ShapeDtypeStruct(o_all.shape, o_all.dtype),
        input_output_aliases={7: 0},
        compiler_params=_cparams(("arbitrary",), VMEM_LIMIT),
        name="attn_sample",
    )(page_table, ql_s, qr_s, lat_new, kr_new, cache_lat, cache_kr_t, o_all)


def _rope_tables(positions, reps):
    half = MLA_ROPE // 2
    inv = ROPE_THETA ** (-jnp.arange(half, dtype=F32) / half)
    ang = positions.astype(F32)[:, None] * inv[None, :]
    cos = jnp.cos(ang)
    sin = jnp.sin(ang)
    cos = jnp.concatenate([cos, cos], axis=1)
    sin = jnp.concatenate([-sin, sin], axis=1)
    return jnp.tile(cos, (1, reps)), jnp.tile(sin, (1, reps))


def kernel(x_prompt, x_sample, cache_mla_latent, cache_mla_krope, state_ssd, state_ssd_conv, state_short_conv, page_table, norm_mix_w, w_in, gate_b, sc_conv_w, sc_w_out, ssd_conv_w, ssd_conv_b, ssd_dt_bias, ssd_A_log, ssd_D, ssd_norm_w, ssd_w_out, mla_q_norm_w, mla_w_q_up, mla_kv_norm_w, mla_w_uk, mla_w_uv, mla_w_o, w_out, norm_mlp_w, mlp_w1, mlp_w2, final_norm_w):
    bp, sp, d = x_prompt.shape
    bs, ss, _ = x_sample.shape
    assert ss == 1
    depth = w_in.shape[0]
    mp = bp * sp
    m = mp + bs
    n_pages = page_table.shape[1]
    page = cache_mla_latent.shape[1]
    past_len = n_pages * page

    sc_dim = sc_conv_w.shape[2]
    ssd_dim = ssd_norm_w.shape[1]
    conv_dim = ssd_conv_w.shape[2]
    n_heads = ssd_dt_bias.shape[1]
    q_lora = mla_q_norm_w.shape[1]
    kv_lora = mla_kv_norm_w.shape[1]
    scale = (MLA_NOPE + MLA_ROPE) ** -0.5

    sizes = (sc_dim, sc_dim, sc_dim, ssd_dim, conv_dim, n_heads, q_lora, kv_lora, MLA_ROPE, 3 * d)
    starts = [0]
    for s in sizes:
        starts.append(starts[-1] + s)
    (o_scb, o_scc, o_scx, o_z, o_xbc, o_dt, o_cq, o_ckv, o_kr, o_gate, o_end) = starts
    w_in_t = jnp.swapaxes(w_in, 1, 2)
    n_proj1 = o_dt
    c_z, c_xbc = o_z, o_xbc
    assert n_proj1 % 512 == 0 and (3 * d) % 512 == 0
    small_w = q_lora + kv_lora + 2 * LANE
    c_dt = q_lora + kv_lora + LANE
    assert small_w % 512 == 0
    w_small = jnp.concatenate(
        [w_in[:, :, o_cq:o_gate], jnp.zeros((depth, d, LANE - MLA_ROPE), F32),
         w_in[:, :, o_dt:o_cq], jnp.zeros((depth, d, LANE - n_heads), F32)], axis=2).astype(BF16)
    wq = mla_w_q_up.reshape(depth, q_lora, MLA_HEADS, MLA_NOPE + MLA_ROPE)
    wq = jnp.concatenate([wq[..., :MLA_NOPE].reshape(depth, q_lora, -1),
                          wq[..., MLA_NOPE:].reshape(depth, q_lora, -1)], axis=2).astype(BF16)
    w_uk_t = jnp.transpose(mla_w_uk, (0, 2, 3, 1)).astype(BF16)
    w_uv_h = jnp.transpose(mla_w_uv, (0, 2, 1, 3)).astype(BF16)
    cache_kr_t = jnp.transpose(cache_mla_krope, (0, 2, 3, 1))
    w2_bf16 = mlp_w2.astype(BF16)

    br = 128
    assert sp % br == 0 and bs == br and mp % br == 0
    positions = jnp.concatenate([jnp.arange(sp), jnp.full((br,), past_len)])
    cos_q, sin_q = _rope_tables(positions, MLA_HEADS)
    cos_k, sin_k = _rope_tables(positions, 1)
    zpad = jnp.zeros((positions.shape[0], LANE - MLA_ROPE), F32)
    cos_k = jnp.concatenate([cos_k, zpad], axis=1)
    sin_k = jnp.concatenate([sin_k, zpad], axis=1)
    n_pblk = mp // br
    tbl_index = lambda i: jnp.where(i < n_pblk, i % (sp // br), sp // br)

    sample_blk = mp // bs
    x = jnp.concatenate([x_prompt.reshape(mp, d), x_sample.reshape(bs, d)], axis=0)
    state_all = state_ssd.reshape(depth, bs, ssd_dim, SSD_STATE)
    gn = SSD_GROUPS * SSD_STATE

    lat_l, kr_l, sc_p, sc_s, cv_p, cv_s, st_p = [], [], [], [], [], [], []
    st_new_all = jnp.zeros(state_all.shape, F32)
    u_sc = jnp.zeros((m, sc_dim), BF16)
    u_ssd = jnp.zeros((m, ssd_dim), BF16)
    o_lat = jnp.zeros((MLA_HEADS, m, kv_lora), BF16)
    for l in range(depth):
        h = rmsnorm_rows(x, norm_mix_w[l], BF16)
        proj = matmul_w(h, w_in_t, l, F32, ncols=n_proj1, w_is_nk=True)
        gates = matmul_w(h, w_in_t, l, F32, col0=o_gate, ncols=3 * d, w_is_nk=True)
        proj_s = matmul_w(h, w_small, l, F32)

        u_sc, sc_tail = sc_conv_prompt(proj, sc_conv_w[l], bp, sp, u_sc)
        u_sc, sc_nbuf = sc_conv_sample(proj, jnp.swapaxes(state_short_conv[l], 0, 1),
                                       sc_conv_w[l], u_sc, sample_blk, bs)
        sc_p.append(sc_tail)
        sc_s.append(jnp.swapaxes(sc_nbuf, 0, 1))

        xbc_p, cv_tail = ssd_conv_prompt(proj, c_xbc, ssd_conv_w[l], ssd_conv_b[l], bp, sp)
        u_ssd, st_new_p = ssd_prompt(xbc_p, proj, c_z, proj_s, c_dt, ssd_dt_bias[l], ssd_A_log[l],
                                     ssd_D[l], ssd_norm_w[l], bp, sp, u_ssd)
        act_s, cv_nbuf, da_t, dtx_t = ssd_sample_pre(
            proj, c_xbc, proj_s, c_dt, jnp.swapaxes(state_ssd_conv[l], 0, 1), ssd_conv_w[l],
            ssd_conv_b[l], ssd_dt_bias[l], ssd_A_log[l], sample_blk, bs, ssd_dim)
        st_new_all, y_t = ssd_sample_step(
            state_all, l, da_t, dtx_t,
            act_s[:, ssd_dim:ssd_dim + gn].reshape(bs, SSD_GROUPS, SSD_STATE),
            act_s[:, ssd_dim + gn:].reshape(bs, SSD_GROUPS, SSD_STATE), st_new_all)
        u_ssd = ssd_sample_post(y_t, act_s, proj, c_z, jnp.repeat(ssd_D[l], SSD_HEADDIM),
                                ssd_norm_w[l], u_ssd, sample_blk, bs)
        cv_p.append(cv_tail)
        cv_s.append(jnp.swapaxes(cv_nbuf, 0, 1))
        st_p.append(st_new_p.reshape(bp, n_heads, SSD_HEADDIM, SSD_STATE))

        cqn, lat, latb, kr, krb = mla_pre(proj_s, 0, small_w, mla_q_norm_w[l],
                                          mla_kv_norm_w[l], cos_k, sin_k, tbl_index)
        q = matmul_w(cqn, wq, l, F32, bn=1024)
        ql = q_latent(q, w_uk_t, l)
        qr = q_rope(q, MLA_HEADS * MLA_NOPE, cos_q, sin_q, tbl_index)
        o_lat = attn_prompt(ql, qr, latb, krb, bp, sp, scale, o_lat)
        o_lat = attn_sample(page_table, jnp.swapaxes(ql[:, mp:], 0, 1),
                            jnp.swapaxes(qr[:, mp:], 0, 1), lat[mp:, None, :], kr[mp:, None, :],
                            cache_mla_latent, cache_kr_t, o_lat, l, scale, sample_blk)
        o_mla = v_up(o_lat, w_uv_h, l)
        lat_l.append(lat)
        kr_l.append(kr)

        merged = branch_merge(u_sc, u_ssd, o_mla, sc_w_out, ssd_w_out, mla_w_o, l,
                              gates, 0, gate_b)
        x = matmul_w(merged, w_out, l, F32, res=x, bn=256)
        h2 = rmsnorm_rows(x, norm_mlp_w[l], BF16)
        a = matmul_w(h2, mlp_w1, l, BF16, act="relu2")
        x = matmul_residual(a, w2_bf16, l, x)

    y_prompt = rmsnorm_rows(x, final_norm_w, F32, row0=0, rows=mp).reshape(bp, sp, d)
    y_sample = rmsnorm_rows(x, final_norm_w, F32, row0=mp, rows=bs).reshape(bs, ss, d)
    lat_all = jnp.stack(lat_l, axis=1)
    kr_all = jnp.stack(kr_l, axis=1)
    return (y_prompt, y_sample,
            lat_all[:mp].reshape(bp, sp, depth, kv_lora), kr_all[:mp].reshape(bp, sp, depth, MLA_ROPE),
            lat_all[mp:].reshape(bs, ss, depth, kv_lora), kr_all[mp:].reshape(bs, ss, depth, MLA_ROPE),
            jnp.stack(st_p, axis=0), st_new_all.reshape(depth, bs, n_heads, SSD_HEADDIM, SSD_STATE),
            jnp.stack(cv_p, axis=0), jnp.stack(cv_s, axis=0),
            jnp.stack(sc_p, axis=0), jnp.stack(sc_s, axis=0))
```
